```python
import math
import jax, jax.numpy as jnp
from jax import lax
import numpy as np

D_MODEL = 4096
BATCH = 4
SEQ = 2048
DEPTH = 1
DEC_BATCH = 128
DEC_SEQ = 8
PAST_LEN = 2048
PAGE_SIZE = 128

N_HEADS = 8
D_HEAD = 128
QK_DIM = 2 * D_HEAD
V_DIM = 2 * D_HEAD
ATTN_WIDTH = N_HEADS * V_DIM
LRU_WIDTH = D_MODEL // 2
LRU_BLOCKS = 16
LRU_BLOCK_DIM = LRU_WIDTH // LRU_BLOCKS
CONV_WIDTH = 4
LRU_C = 8.0
NUM_BUCKETS = 32
MAX_DISTANCE = 128
FFN_HIDDEN = ((8 * D_MODEL // 3 + 255) // 256) * 256
Q_BLOCK = 128
EPS = 1e-6
NEG_INF = -1e30
IN_COLS = 2 * N_HEADS * QK_DIM + ATTN_WIDTH + 2 * LRU_WIDTH + 2 * D_MODEL

kernel_name = "hybrid_diffattn_rglru_gated_decode_step"


def rmsnorm(x, g):
    xf = x.astype(jnp.float32)
    xf = xf * lax.rsqrt(jnp.mean(xf * xf, axis=-1, keepdims=True) + EPS)
    return (xf * g.astype(jnp.float32)).astype(x.dtype)


def lambda_init(layer):
    return 0.8 - 0.6 * math.exp(-0.3 * layer)


def t5_bias(q_pos, k_pos, rel_bias):
    n = jnp.maximum(q_pos[:, None] - k_pos[None, :], 0)
    max_exact = NUM_BUCKETS // 2
    nf = jnp.maximum(n, max_exact).astype(jnp.float32)
    large = max_exact + (jnp.log(nf / max_exact) / math.log(MAX_DISTANCE / max_exact)
                         * (NUM_BUCKETS - max_exact)).astype(jnp.int32)
    bucket = jnp.where(n < max_exact, n, jnp.minimum(large, NUM_BUCKETS - 1))
    return jnp.transpose(rel_bias[bucket].astype(jnp.float32), (2, 0, 1))


def diff_attend(q1, q2, k1, k2, v, bias, mask, lam):
    scale = D_HEAD ** -0.5
    s1 = jnp.einsum('bqhd,bkhd->bhqk', q1, k1).astype(jnp.float32) * scale + bias
    s2 = jnp.einsum('bqhd,bkhd->bhqk', q2, k2).astype(jnp.float32) * scale + bias
    p1 = jax.nn.softmax(jnp.where(mask, s1, NEG_INF), axis=-1)
    p2 = jax.nn.softmax(jnp.where(mask, s2, NEG_INF), axis=-1)
    w = (p1 - lam * p2).astype(v.dtype)
    return jnp.einsum('bhqk,bkhv->bqhv', w, v)


def split_proj(xn, w_in, q_norm, k_norm):
    b, t, _ = xn.shape
    proj = jnp.einsum('btd,dc->btc', xn, w_in)
    qk = N_HEADS * QK_DIM
    o1 = 2 * qk
    o2 = o1 + ATTN_WIDTH
    o3 = o2 + LRU_WIDTH
    o4 = o3 + LRU_WIDTH
    q = rmsnorm(proj[..., :qk].reshape(b, t, N_HEADS, 2, D_HEAD), q_norm)
    k = rmsnorm(proj[..., qk:o1].reshape(b, t, N_HEADS, 2, D_HEAD), k_norm)
    v = proj[..., o1:o2].reshape(b, t, N_HEADS, V_DIM)
    return q, k, v, proj[..., o2:o3], proj[..., o3:o4], proj[..., o4:]


def prompt_attention(q, k, v, rel_bias, lam):
    b, t = q.shape[0], q.shape[1]
    nb = t // Q_BLOCK
    qb = q.reshape(b, nb, Q_BLOCK, N_HEADS, 2, D_HEAD).transpose(1, 0, 2, 3, 4, 5)
    k_pos = jnp.arange(t)
    k1 = k[..., 0, :]
    k2 = k[..., 1, :]

    def block(args):
        qi, bi = args
        q_pos = bi * Q_BLOCK + jnp.arange(Q_BLOCK)
        bias = t5_bias(q_pos, k_pos, rel_bias)
        mask = k_pos[None, :] <= q_pos[:, None]
        return diff_attend(qi[..., 0, :], qi[..., 1, :], k1, k2, v, bias, mask, lam)

    o = lax.map(block, (qb, jnp.arange(nb)))
    return o.transpose(1, 0, 2, 3, 4).reshape(b, t, N_HEADS, V_DIM)


def sample_attention(q, k, v, cache_k, cache_v, layer, page_table, rel_bias, lam):
    t = q.shape[1]
    past_len = page_table.shape[1] * cache_k.shape[2]
    q_pos = past_len + jnp.arange(t)
    k_pos = jnp.arange(past_len + t)
    bias = t5_bias(q_pos, k_pos, rel_bias)
    mask = k_pos[None, :] <= q_pos[:, None]

    def one_seq(args):
        qs, ks, vs, pt = args
        kp = cache_k[layer, pt].reshape(past_len, N_HEADS, 2, D_HEAD)
        vp = cache_v[layer, pt].reshape(past_len, N_HEADS, V_DIM)
        ka = jnp.concatenate([kp, ks.astype(kp.dtype)], axis=0)[None]
        va = jnp.concatenate([vp, vs.astype(vp.dtype)], axis=0)[None]
        return diff_attend(qs[None, ..., 0, :], qs[None, ..., 1, :],
                           ka[..., 0, :], ka[..., 1, :], va, bias, mask, lam)[0]

    return lax.map(one_seq, (q, k, v, page_table))


def lru_combine(c1, c2):
    a1, b1 = c1
    a2, b2 = c2
    return a1 * a2, a2 * b1 + b2


def rglru_branch(xr, xg, conv_buf, h0, conv_w, conv_b, w_r, b_r, w_i, b_i, lru_lambda):
    b, t, _ = xr.shape
    xpad = jnp.concatenate([conv_buf.astype(xr.dtype), xr], axis=1)
    xc = conv_b + sum(conv_w[j] * xpad[:, j:j + t] for j in range(CONV_WIDTH))
    xb = xc.reshape(b, t, LRU_BLOCKS, LRU_BLOCK_DIM)
    r = jax.nn.sigmoid((jnp.einsum('btnc,ncd->btnd', xb, w_r).reshape(b, t, LRU_WIDTH) + b_r).astype(jnp.float32))
    i = jax.nn.sigmoid((jnp.einsum('btnc,ncd->btnd', xb, w_i).reshape(b, t, LRU_WIDTH) + b_i).astype(jnp.float32))
    log_a = -LRU_C * r * jax.nn.softplus(-lru_lambda.astype(jnp.float32))
    a = jnp.exp(log_a)
    u = jnp.sqrt(-jnp.expm1(2.0 * log_a)) * i * xc.astype(jnp.float32)
    u = u.at[:, 0].add(a[:, 0] * h0.astype(jnp.float32))
    _, h = lax.associative_scan(lru_combine, (a, u), axis=1)
    y = h.astype(xr.dtype) * jax.nn.gelu(xg)
    return y, h[:, -1], xpad[:, -(CONV_WIDTH - 1):]


def merge_branches(o_attn, y_lru, gpre, b_gate, attn_subln, w_attn_o, w_lru_o, w_out, lam_init):
    b, t = o_attn.shape[0], o_attn.shape[1]
    o = rmsnorm(o_attn, attn_subln) * (1.0 - lam_init)
    a_out = jnp.einsum('btc,cd->btd', o.reshape(b, t, ATTN_WIDTH), w_attn_o)
    r_out = jnp.einsum('btc,cd->btd', y_lru, w_lru_o)
    g = jax.nn.sigmoid(gpre + b_gate)
    merged = g[..., :D_MODEL] * a_out + g[..., D_MODEL:] * r_out
    return jnp.einsum('btd,de->bte', merged, w_out)


def swiglu(x, wg, wu, wd):
    hid = jax.nn.silu(jnp.einsum('btd,df->btf', x, wg)) * jnp.einsum('btd,df->btf', x, wu)
    return jnp.einsum('btf,fd->btd', hid, wd)


def setup_inputs(seed: int = 0) -> dict:
    key = jax.random.key(seed)
    ks = jax.random.split(key, 32)
    f32 = jnp.float32
    n_pages = PAST_LEN // PAGE_SIZE
    n_phys = (DEC_BATCH * n_pages * 5) // 4
    perm = jax.random.permutation(ks[0], n_phys)
    page_table = perm[:DEC_BATCH * n_pages].reshape(DEC_BATCH, n_pages).astype(jnp.int32)
    u = jax.random.uniform(ks[20], (DEPTH, LRU_WIDTH), f32, 0.9, 0.999)
    a0 = u ** (1.0 / LRU_C)
    lru_lambda = jnp.log(a0) - jnp.log1p(-a0)
    nrm = lambda k, shape, s: jax.random.normal(k, shape, f32) * s
    return {
        "x_prompt": nrm(ks[1], (BATCH, SEQ, D_MODEL), 1.0),
        "x_sample": nrm(ks[2], (DEC_BATCH, DEC_SEQ, D_MODEL), 1.0),
        "cache_k": nrm(ks[3], (DEPTH, n_phys, PAGE_SIZE, N_HEADS, QK_DIM), 1.0),
        "cache_v": nrm(ks[4], (DEPTH, n_phys, PAGE_SIZE, N_HEADS, V_DIM), 1.0),
        "state_h": nrm(ks[5], (DEPTH, DEC_BATCH, LRU_WIDTH), 0.5),
        "state_conv": nrm(ks[6], (DEPTH, DEC_BATCH, CONV_WIDTH - 1, LRU_WIDTH), 1.0),
        "page_table": page_table,
        "norm_mix": 1.0 + nrm(ks[7], (DEPTH, D_MODEL), 0.02),
        "w_in": nrm(ks[8], (DEPTH, D_MODEL, IN_COLS), D_MODEL ** -0.5),
        "b_gate": nrm(ks[9], (DEPTH, 2 * D_MODEL), 0.02),
        "q_norm": 1.0 + nrm(ks[10], (DEPTH, D_HEAD), 0.02),
        "k_norm": 1.0 + nrm(ks[11], (DEPTH, D_HEAD), 0.02),
        "lambda_q1": nrm(ks[12], (DEPTH, D_HEAD), 0.1),
        "lambda_k1": nrm(ks[13], (DEPTH, D_HEAD), 0.1),
        "lambda_q2": nrm(ks[14], (DEPTH, D_HEAD), 0.1),
        "lambda_k2": nrm(ks[15], (DEPTH, D_HEAD), 0.1),
        "rel_bias": nrm(ks[16], (NUM_BUCKETS, N_HEADS), 0.5),
        "attn_subln": 1.0 + nrm(ks[17], (DEPTH, N_HEADS, V_DIM), 0.02),
        "w_attn_o": nrm(ks[18], (DEPTH, ATTN_WIDTH, D_MODEL), ATTN_WIDTH ** -0.5),
        "conv_w": nrm(ks[19], (DEPTH, CONV_WIDTH, LRU_WIDTH), CONV_WIDTH ** -0.5),
        "conv_b": nrm(ks[21], (DEPTH, LRU_WIDTH), 0.02),
        "w_r": nrm(ks[22], (DEPTH, LRU_BLOCKS, LRU_BLOCK_DIM, LRU_BLOCK_DIM), LRU_BLOCK_DIM ** -0.5),
        "b_r": nrm(ks[23], (DEPTH, LRU_WIDTH), 0.02),
        "w_i": nrm(ks[24], (DEPTH, LRU_BLOCKS, LRU_BLOCK_DIM, LRU_BLOCK_DIM), LRU_BLOCK_DIM ** -0.5),
        "b_i": nrm(ks[25], (DEPTH, LRU_WIDTH), 0.02),
        "lru_lambda": lru_lambda,
        "w_lru_o": nrm(ks[26], (DEPTH, LRU_WIDTH, D_MODEL), LRU_WIDTH ** -0.5),
        "w_out": nrm(ks[27], (DEPTH, D_MODEL, D_MODEL), D_MODEL ** -0.5),
        "norm_ffn": 1.0 + nrm(ks[28], (DEPTH, D_MODEL), 0.02),
        "w_ffn_gate": nrm(ks[29], (DEPTH, D_MODEL, FFN_HIDDEN), D_MODEL ** -0.5),
        "w_ffn_up": nrm(ks[30], (DEPTH, D_MODEL, FFN_HIDDEN), D_MODEL ** -0.5),
        "w_ffn_down": nrm(ks[31], (DEPTH, FFN_HIDDEN, D_MODEL), FFN_HIDDEN ** -0.5),
    }


def reference(x_prompt, x_sample, cache_k, cache_v, state_h, state_conv, page_table,
              norm_mix, w_in, b_gate, q_norm, k_norm, lambda_q1, lambda_k1, lambda_q2, lambda_k2,
              rel_bias, attn_subln, w_attn_o, conv_w, conv_b, w_r, b_r, w_i, b_i, lru_lambda,
              w_lru_o, w_out, norm_ffn, w_ffn_gate, w_ffn_up, w_ffn_down):
    f32 = jnp.float32
    xp = x_prompt
    xs = x_sample
    bp, tp = xp.shape[0], xp.shape[1]
    bs, ts = xs.shape[0], xs.shape[1]
    kp_l, vp_l, hp_l, cp_l = [], [], [], []
    ks_l, vs_l, hs_l, cs_l = [], [], [], []
    for l in range(DEPTH):
        lam_init = lambda_init(l)
        lam = (jnp.exp(jnp.sum(lambda_q1[l].astype(f32) * lambda_k1[l].astype(f32)))
               - jnp.exp(jnp.sum(lambda_q2[l].astype(f32) * lambda_k2[l].astype(f32))) + lam_init)
        lru_w = (conv_w[l], conv_b[l], w_r[l], b_r[l], w_i[l], b_i[l], lru_lambda[l])

        xn = rmsnorm(xp, norm_mix[l])
        q, k, v, xr, xg, gpre = split_proj(xn, w_in[l], q_norm[l], k_norm[l])
        o_attn = prompt_attention(q, k, v, rel_bias, lam)
        conv0 = jnp.zeros((bp, CONV_WIDTH - 1, LRU_WIDTH), xr.dtype)
        h0 = jnp.zeros((bp, LRU_WIDTH), f32)
        y_lru, h_fin, conv_new = rglru_branch(xr, xg, conv0, h0, *lru_w)
        xp = xp + merge_branches(o_attn, y_lru, gpre, b_gate[l], attn_subln[l],
                                 w_attn_o[l], w_lru_o[l], w_out[l], lam_init)
        xp = xp + swiglu(rmsnorm(xp, norm_ffn[l]), w_ffn_gate[l], w_ffn_up[l], w_ffn_down[l])
        kp_l.append(k.reshape(bp, tp, N_HEADS, QK_DIM))
        vp_l.append(v)
        hp_l.append(h_fin.astype(state_h.dtype))
        cp_l.append(conv_new.astype(state_conv.dtype))

        xn = rmsnorm(xs, norm_mix[l])
        q, k, v, xr, xg, gpre = split_proj(xn, w_in[l], q_norm[l], k_norm[l])
        o_attn = sample_attention(q, k, v, cache_k, cache_v, l, page_table, rel_bias, lam)
        y_lru, h_fin, conv_new = rglru_branch(xr, xg, state_conv[l], state_h[l], *lru_w)
        xs = xs + merge_branches(o_attn, y_lru, gpre, b_gate[l], attn_subln[l],
                                 w_attn_o[l], w_lru_o[l], w_out[l], lam_init)
        xs = xs + swiglu(rmsnorm(xs, norm_ffn[l]), w_ffn_gate[l], w_ffn_up[l], w_ffn_down[l])
        ks_l.append(k.reshape(bs, ts, N_HEADS, QK_DIM))
        vs_l.append(v)
        hs_l.append(h_fin.astype(state_h.dtype))
        cs_l.append(conv_new.astype(state_conv.dtype))

    k_prompt = jnp.stack(kp_l)
    v_prompt = jnp.stack(vp_l)
    h_prompt = jnp.stack(hp_l)
    conv_prompt = jnp.stack(cp_l)
    k_sample = jnp.stack(ks_l)
    v_sample = jnp.stack(vs_l)
    h_sample = jnp.stack(hs_l)
    conv_sample = jnp.stack(cs_l)
    return (xp, xs, k_prompt, v_prompt, h_prompt, conv_prompt, k_sample, v_sample, h_sample, conv_sample)
```

```python
import functools
import math

import jax
import jax.numpy as jnp
from jax import lax
from jax.experimental import pallas as pl
from jax.experimental.pallas import tpu as pltpu

F32 = jnp.float32
BF16 = jnp.bfloat16

N_HEADS = 8
D_HEAD = 128
HEAD_W = 2 * D_HEAD
LRU_BLOCK_DIM = 128
CONV_WIDTH = 4
LRU_C = 8.0
NUM_BUCKETS = 32
MAX_DISTANCE = 128
EPS = 1e-6
NEG_INF = -1e30
LAYER = 0
LAM_INIT = 0.8 - 0.6 * math.exp(-0.3 * LAYER)

VMEM_LIMIT_BYTES = 56 * 1024 * 1024

ROW_TILE = 1024
NORM_ROWS = 256
ATTN_TQ = 256
ATTN_TK = 256
PAGES_PER_STEP = 4
LRU_ROWS = 256
LRU_LANES = 256


def _params(*semantics):
    return pltpu.CompilerParams(dimension_semantics=semantics,
                                vmem_limit_bytes=VMEM_LIMIT_BYTES)


def _rmsnorm_kernel(x_ref, g_ref, o_ref):
    x = x_ref[...]
    ms = jnp.mean(x * x, axis=-1, keepdims=True)
    o_ref[...] = (x * lax.rsqrt(ms + EPS) * g_ref[...]).astype(o_ref.dtype)


def _rmsnorm(x, g):
    m, d = x.shape
    return pl.pallas_call(
        _rmsnorm_kernel,
        out_shape=jax.ShapeDtypeStruct((m, d), BF16),
        grid=(m // NORM_ROWS,),
        in_specs=[pl.BlockSpec((NORM_ROWS, d), lambda i: (i, 0)),
                  pl.BlockSpec((1, d), lambda i: (0, 0))],
        out_specs=pl.BlockSpec((NORM_ROWS, d), lambda i: (i, 0)),
        compiler_params=_params("parallel"),
        name="rmsnorm",
    )(x, g.reshape(1, d))


def _in_proj_kernel(x_ref, w_ref, g_ref, o_ref, *, n_norm_tiles):
    j = pl.program_id(1)
    acc = jnp.dot(x_ref[...], w_ref[...], preferred_element_type=F32)

    @pl.when(j < n_norm_tiles)
    def _():
        for c in range(0, acc.shape[1], D_HEAD):
            seg = acc[:, c:c + D_HEAD]
            ms = jnp.mean(seg * seg, axis=-1, keepdims=True)
            o_ref[:, c:c + D_HEAD] = seg * lax.rsqrt(ms + EPS) * g_ref[:, c:c + D_HEAD]

    @pl.when(j >= n_norm_tiles)
    def _():
        o_ref[...] = acc


def _in_proj(xn, w, qk_gain, *, tn=512):
    m, d = xn.shape
    n = w.shape[1]
    n_norm = qk_gain.shape[1]
    n_norm_tiles = n_norm // tn
    return pl.pallas_call(
        functools.partial(_in_proj_kernel, n_norm_tiles=n_norm_tiles),
        out_shape=jax.ShapeDtypeStruct((m, n), F32),
        grid=(m // ROW_TILE, n // tn),
        in_specs=[pl.BlockSpec((ROW_TILE, d), lambda i, j: (i, 0)),
                  pl.BlockSpec((d, tn), lambda i, j: (0, j)),
                  pl.BlockSpec((1, tn), lambda i, j: (0, jnp.minimum(j, n_norm_tiles - 1)))],
        out_specs=pl.BlockSpec((ROW_TILE, tn), lambda i, j: (i, j)),
        compiler_params=_params("parallel", "arbitrary"),
        name="in_proj",
    )(xn, w, qk_gain)


def _diff_lambda(lq1, lk1, lq2, lk2):
    s1 = jnp.sum(lq1[...] * lk1[...], axis=-1, keepdims=True)
    s2 = jnp.sum(lq2[...] * lk2[...], axis=-1, keepdims=True)
    return jnp.exp(s1) - jnp.exp(s2) + LAM_INIT


def _head_subnorm(o, g):
    ms = jnp.mean(o * o, axis=-1, keepdims=True)
    return o * lax.rsqrt(ms + EPS) * g * (1.0 - LAM_INIT)


def _online_softmax_step(s, v_blk, m_ref, l_ref, acc_ref):
    m_old = m_ref[...]
    m_new = jnp.maximum(m_old, jnp.max(s, axis=-1, keepdims=True))
    alpha = jnp.exp(m_old - m_new)
    p = jnp.exp(s - m_new)
    l_ref[...] = alpha * l_ref[...] + jnp.sum(p, axis=-1, keepdims=True)
    acc_ref[...] = alpha * acc_ref[...] + jnp.dot(p.astype(BF16), v_blk,
                                                  preferred_element_type=F32)
    m_ref[...] = m_new


def _t5_bias(dist, rel_bias):
    n = jnp.maximum(dist, 0)
    max_exact = NUM_BUCKETS // 2
    nf = jnp.maximum(n, max_exact).astype(F32)
    large = max_exact + (jnp.log(nf / max_exact) / math.log(MAX_DISTANCE / max_exact)
                         * (NUM_BUCKETS - max_exact)).astype(jnp.int32)
    bucket = jnp.where(n < max_exact, n, jnp.minimum(large, NUM_BUCKETS - 1))
    return rel_bias[bucket].astype(F32)


def _prompt_attn_kernel(q_ref, k_ref, v_ref, bias_ref, bfar_ref, g_ref, lq1, lk1, lq2, lk2,
                        o_ref, kb, vb, m1, l1, a1, m2, l2, a2):
    qi = pl.program_id(2)
    tq = q_ref.shape[0]
    tk = ATTN_TK
    scale = D_HEAD ** -0.5

    @pl.when(qi == 0)
    def _():
        kb[...] = k_ref[...].astype(BF16)
        vb[...] = v_ref[...].astype(BF16)

    q = q_ref[...]
    q1 = q[:, :D_HEAD].astype(BF16)
    q2 = q[:, D_HEAD:].astype(BF16)

    for m_ref, l_ref, a_ref in ((m1, l1, a1), (m2, l2, a2)):
        m_ref[...] = jnp.full(m_ref.shape, NEG_INF, F32)
        l_ref[...] = jnp.zeros(l_ref.shape, F32)
        a_ref[...] = jnp.zeros(a_ref.shape, F32)

    def chunk(kc, bias, mask):
        start = pl.multiple_of(kc * tk, tk)
        k_blk = kb[pl.ds(start, tk), :]
        v_blk = vb[pl.ds(start, tk), :]
        for qh, off, m_ref, l_ref, a_ref in ((q1, 0, m1, l1, a1), (q2, D_HEAD, m2, l2, a2)):
            s = lax.dot_general(qh, k_blk[:, off:off + D_HEAD], (((1,), (1,)), ((), ())),
                                preferred_element_type=F32) * scale + bias
            if mask is not None:
                s = jnp.where(mask, s, NEG_INF)
            _online_softmax_step(s, v_blk, m_ref, l_ref, a_ref)

    row = lax.broadcasted_iota(jnp.int32, (tq, tk), 0)
    col = lax.broadcasted_iota(jnp.int32, (tq, tk), 1)
    chunk(qi, bias_ref[0], col <= row)

    @pl.when(qi >= 1)
    def _():
        chunk(qi - 1, bias_ref[1], None)

    def far(kc, carry):
        chunk(kc, bfar_ref[...], None)
        return carry
    lax.fori_loop(0, jnp.maximum(qi - 1, 0), far, 0)

    lam = _diff_lambda(lq1, lk1, lq2, lk2)
    o = a1[...] / l1[...] - lam * (a2[...] / l2[...])
    o_ref[...] = _head_subnorm(o, g_ref[...]).astype(o_ref.dtype)


def _prompt_attention(proj, n_batch, seq, bias_tiles, bias_far, subln, lams):
    nq = seq // ATTN_TQ
    k_col0 = N_HEADS
    v_col0 = 2 * N_HEADS
    vec = pl.BlockSpec((1, D_HEAD), lambda b, h, i: (0, 0))
    return pl.pallas_call(
        _prompt_attn_kernel,
        out_shape=jax.ShapeDtypeStruct((n_batch * seq, N_HEADS * HEAD_W), BF16),
        grid=(n_batch, N_HEADS, nq),
        in_specs=[pl.BlockSpec((ATTN_TQ, HEAD_W), lambda b, h, i: (b * nq + i, h)),
                  pl.BlockSpec((seq, HEAD_W), lambda b, h, i: (b, k_col0 + h)),
                  pl.BlockSpec((seq, HEAD_W), lambda b, h, i: (b, v_col0 + h)),
                  pl.BlockSpec((None, 2, ATTN_TQ, ATTN_TK), lambda b, h, i: (h, 0, 0, 0)),
                  pl.BlockSpec((None, 1, ATTN_TK), lambda b, h, i: (h, 0, 0)),
                  pl.BlockSpec((None, 1, HEAD_W), lambda b, h, i: (h, 0, 0)),
                  vec, vec, vec, vec],
        out_specs=pl.BlockSpec((ATTN_TQ, HEAD_W), lambda b, h, i: (b * nq + i, h)),
        scratch_shapes=[pltpu.VMEM((seq, HEAD_W), BF16), pltpu.VMEM((seq, HEAD_W), BF16),
                        pltpu.VMEM((ATTN_TQ, 1), F32), pltpu.VMEM((ATTN_TQ, 1), F32),
                        pltpu.VMEM((ATTN_TQ, HEAD_W), F32),
                        pltpu.VMEM((ATTN_TQ, 1), F32), pltpu.VMEM((ATTN_TQ, 1), F32),
                        pltpu.VMEM((ATTN_TQ, HEAD_W), F32)],
        compiler_params=_params("parallel", "parallel", "arbitrary"),
        name="prompt_attention",
    )(proj, proj, proj, bias_tiles, bias_far, subln, *lams)


def _sample_attn_kernel(pt_ref, q_ref, kn_ref, vn_ref, *rest):
    npg = PAGES_PER_STEP
    k_refs = rest[:npg]
    v_refs = rest[npg:2 * npg]
    (bias_ref, bias_new_ref, g_ref, lq1, lk1, lq2, lk2,
     o_ref, qmt, m_ref, l_ref, acc_ref) = rest[2 * npg:]
    del pt_ref
    p = pl.program_id(1)
    n_q = q_ref.shape[0]
    scale = D_HEAD ** -0.5

    def update(k2d, v2d, bias):
        s = lax.dot_general(qmt[...], k2d, (((1,), (1,)), ((), ())),
                            preferred_element_type=F32) * scale + bias
        _online_softmax_step(s, v2d, m_ref, l_ref, acc_ref)

    @pl.when(p == 0)
    def _():
        q = q_ref[...]
        lane = lax.broadcasted_iota(jnp.int32, q.shape, 1)
        qmt[...] = jnp.concatenate([jnp.where(lane < D_HEAD, q, 0.0),
                                    jnp.where(lane >= D_HEAD, q, 0.0)], axis=0).astype(BF16)
        m_ref[...] = jnp.full(m_ref.shape, NEG_INF, F32)
        l_ref[...] = jnp.zeros(l_ref.shape, F32)
        acc_ref[...] = jnp.zeros(acc_ref.shape, F32)
        pad = jnp.zeros((bias_new_ref.shape[1] - n_q, HEAD_W), F32)
        kn = jnp.concatenate([kn_ref[...], pad], axis=0).astype(BF16)
        vn = jnp.concatenate([vn_ref[...], pad], axis=0).astype(BF16)
        update(kn, vn, bias_new_ref[...])

    for i in range(npg):
        pos, heads, w = k_refs[i].shape
        update(k_refs[i][...].reshape(pos * heads, w).astype(BF16),
               v_refs[i][...].reshape(pos * heads, w).astype(BF16), bias_ref[i])

    @pl.when(p == pl.num_programs(1) - 1)
    def _():
        lam = _diff_lambda(lq1, lk1, lq2, lk2)
        o = (acc_ref[0:n_q, :] / l_ref[0:n_q, :]
             - lam * (acc_ref[n_q:2 * n_q, :] / l_ref[n_q:2 * n_q, :]))
        o_ref[...] = _head_subnorm(o, g_ref[...]).astype(o_ref.dtype)


def _sample_attention(q2d, kn2d, vn2d, n_seq, cache_k, cache_v, layer, page_table,
                      bias_pages, bias_new, subln_rows, lams):
    page, heads, w = cache_k.shape[2:]
    n_q = q2d.shape[0] // n_seq
    n_pages = page_table.shape[1]
    npg = PAGES_PER_STEP
    rows = 2 * n_q

    def page_spec(i):
        return pl.BlockSpec((None, None, page, heads, w),
                            lambda s, p, pt: (layer, pt[s, p * npg + i], 0, 0, 0))

    seq_rows = pl.BlockSpec((n_q, w), lambda s, p, pt: (s, 0))
    vec = pl.BlockSpec((1, D_HEAD), lambda s, p, pt: (0, 0))
    in_specs = ([seq_rows, seq_rows, seq_rows]
                + [page_spec(i) for i in range(npg)]
                + [page_spec(i) for i in range(npg)]
                + [pl.BlockSpec((npg, rows, page * heads), lambda s, p, pt: (p, 0, 0)),
                   pl.BlockSpec(bias_new.shape, lambda s, p, pt: (0, 0)),
                   pl.BlockSpec((n_q, w), lambda s, p, pt: (0, 0)),
                   vec, vec, vec, vec])
    return pl.pallas_call(
        _sample_attn_kernel,
        out_shape=jax.ShapeDtypeStruct((n_seq * n_q, w), BF16),
        grid_spec=pltpu.PrefetchScalarGridSpec(
            num_scalar_prefetch=1,
            grid=(n_seq, n_pages // npg),
            in_specs=in_specs,
            out_specs=seq_rows,
            scratch_shapes=[pltpu.VMEM((rows, w), BF16),
                            pltpu.VMEM((rows, 1), F32), pltpu.VMEM((rows, 1), F32),
                            pltpu.VMEM((rows, w), F32)]),
        compiler_params=_params("parallel", "arbitrary"),
        name="sample_attention",
    )(page_table, q2d, kn2d, vn2d, *([cache_k] * npg), *([cache_v] * npg),
      bias_pages, bias_new, subln_rows, *lams)


def _gelu_tanh(x):
    return 0.5 * x * (1.0 + jnp.tanh(math.sqrt(2.0 / math.pi) * (x + 0.044715 * (x * x * x))))


def _softplus(x):
    return jnp.maximum(x, 0.0) + jnp.log1p(jnp.exp(-jnp.abs(x)))


def _expm1(x):
    u = jnp.exp(x)
    um1 = u - 1.0
    return jnp.where(u == 1.0, x, jnp.where(um1 == -1.0, -1.0, um1 * x / jnp.log(u)))


def _lru_gates(xc, wr_ref, br_ref, wi_ref, bi_ref, lam_ref):
    r_parts, i_parts = [], []
    for n in range(xc.shape[1] // LRU_BLOCK_DIM):
        xb = xc[:, n * LRU_BLOCK_DIM:(n + 1) * LRU_BLOCK_DIM].astype(BF16)
        r_parts.append(jnp.dot(xb, wr_ref[n], preferred_element_type=F32))
        i_parts.append(jnp.dot(xb, wi_ref[n], preferred_element_type=F32))
    r = jax.nn.sigmoid(jnp.concatenate(r_parts, axis=1) + br_ref[...])
    i = jax.nn.sigmoid(jnp.concatenate(i_parts, axis=1) + bi_ref[...])
    log_a = -LRU_C * r * _softplus(-lam_ref[...])
    a = jnp.exp(log_a)
    u = jnp.sqrt(-_expm1(2.0 * log_a)) * i * xc
    return a, u


def _segmented_scan(a, u, tmod, seg_len):
    rows = a.shape[0]
    s = 1
    while s < seg_len:
        keep = tmod >= s
        a_prev = jnp.where(keep, pltpu.roll(a, s, 0), 1.0)
        u_prev = jnp.where(keep, pltpu.roll(u, s, 0), 0.0)
        u = a * u_prev + u
        a = a * a_prev
        s *= 2
    del rows
    return a, u


def _conv_taps(x, tmod, prev_for_shift, cw_ref, cb_ref):
    xc = cb_ref[...] + cw_ref[CONV_WIDTH - 1:CONV_WIDTH, :] * x
    for k in range(1, CONV_WIDTH):
        shifted = jnp.where(tmod >= k, pltpu.roll(x, k, 0), prev_for_shift(k))
        xc = xc + cw_ref[CONV_WIDTH - 1 - k:CONV_WIDTH - k, :] * shifted
    return xc


def _lru_prompt_kernel(x_ref, xg_ref, cw_ref, cb_ref, wr_ref, br_ref, wi_ref, bi_ref, lam_ref,
                       y_ref, hlast_ref, prev_ref, h_ref):
    t = pl.program_id(2)
    rows, lanes = x_ref.shape

    @pl.when(t == 0)
    def _():
        prev_ref[...] = jnp.zeros(prev_ref.shape, F32)
        h_ref[...] = jnp.zeros(h_ref.shape, F32)

    x = x_ref[...]
    tmod = lax.broadcasted_iota(jnp.int32, (rows, lanes), 0)
    prev = prev_ref[...]

    def prev_for_shift(k):
        head = pltpu.roll(prev, k, 0)
        return jnp.concatenate([head, jnp.zeros((rows - 8, lanes), F32)], axis=0)

    xc = _conv_taps(x, tmod, prev_for_shift, cw_ref, cb_ref)
    a, u = _lru_gates(xc, wr_ref, br_ref, wi_ref, bi_ref, lam_ref)
    a_cum, b_cum = _segmented_scan(a, u, tmod, rows)
    h = a_cum * h_ref[0:1, :] + b_cum
    y_ref[...] = (h * _gelu_tanh(xg_ref[...])).astype(y_ref.dtype)
    h_last = h[rows - 1:rows, :]
    h_ref[...] = jnp.broadcast_to(h_last, h_ref.shape)
    hlast_ref[...] = h_last
    prev_ref[...] = x[rows - 8:rows, :]


def _lru_sample_kernel(x_ref, xg_ref, prev_ref, h0_ref, cw_ref, cb_ref, wr_ref, br_ref, wi_ref,
                       bi_ref, lam_ref, y_ref, h_out_ref, *, seg_len):
    rows, lanes = x_ref.shape
    x = x_ref[...]
    tmod = lax.broadcasted_iota(jnp.int32, (rows, lanes), 0) % seg_len
    prev = prev_ref[...]

    def prev_for_shift(k):
        return pltpu.roll(prev, rows + k - seg_len, 0)

    xc = _conv_taps(x, tmod, prev_for_shift, cw_ref, cb_ref)
    a, u = _lru_gates(xc, wr_ref, br_ref, wi_ref, bi_ref, lam_ref)
    a_cum, b_cum = _segmented_scan(a, u, tmod, seg_len)
    h = a_cum * h0_ref[...] + b_cum
    y_ref[...] = (h * _gelu_tanh(xg_ref[...])).astype(y_ref.dtype)
    h_out_ref[...] = h


def _lru_weight_specs(idx):
    nb = LRU_LANES // LRU_BLOCK_DIM
    vec = lambda rws: pl.BlockSpec((rws, LRU_LANES), lambda *g: (0, idx(*g)))
    mat = pl.BlockSpec((nb, LRU_BLOCK_DIM, LRU_BLOCK_DIM), lambda *g: (idx(*g), 0, 0))
    return [vec(CONV_WIDTH), vec(1), mat, vec(1), mat, vec(1), vec(1)]


def _lru_prompt(proj, xr_col0, xg_col0, n_batch, seq, weights):
    width = weights[1].shape[1]
    nt = seq // LRU_ROWS
    nl = width // LRU_LANES
    xr_blk, xg_blk = xr_col0 // LRU_LANES, xg_col0 // LRU_LANES
    y, h_last = pl.pallas_call(
        _lru_prompt_kernel,
        out_shape=(jax.ShapeDtypeStruct((n_batch * seq, width), BF16),
                   jax.ShapeDtypeStruct((n_batch, 1, width), F32)),
        grid=(n_batch, nl, nt),
        in_specs=[pl.BlockSpec((LRU_ROWS, LRU_LANES), lambda b, j, t: (b * nt + t, xr_blk + j)),
                  pl.BlockSpec((LRU_ROWS, LRU_LANES), lambda b, j, t: (b * nt + t, xg_blk + j))]
                 + _lru_weight_specs(lambda b, j, t: j),
        out_specs=(pl.BlockSpec((LRU_ROWS, LRU_LANES), lambda b, j, t: (b * nt + t, j)),
                   pl.BlockSpec((None, 1, LRU_LANES), lambda b, j, t: (b, 0, j))),
        scratch_shapes=[pltpu.VMEM((8, LRU_LANES), F32), pltpu.VMEM((8, LRU_LANES), F32)],
        compiler_params=_params("parallel", "parallel", "arbitrary"),
        name="lru_prompt",
    )(proj, proj, *weights)
    return y, h_last.reshape(n_batch, width)


def _lru_sample(proj, row0, xr_col0, xg_col0, n_rows, seg_len, prev_rows, h0_rows, weights):
    width = weights[1].shape[1]
    nr = n_rows // LRU_ROWS
    nl = width // LRU_LANES
    r_blk = row0 // LRU_ROWS
    xr_blk, xg_blk = xr_col0 // LRU_LANES, xg_col0 // LRU_LANES
    tile = lambda: pl.BlockSpec((LRU_ROWS, LRU_LANES), lambda i, j: (i, j))
    return pl.pallas_call(
        functools.partial(_lru_sample_kernel, seg_len=seg_len),
        out_shape=(jax.ShapeDtypeStruct((n_rows, width), BF16),
                   jax.ShapeDtypeStruct((n_rows, width), F32)),
        grid=(nr, nl),
        in_specs=[pl.BlockSpec((LRU_ROWS, LRU_LANES), lambda i, j: (r_blk + i, xr_blk + j)),
                  pl.BlockSpec((LRU_ROWS, LRU_LANES), lambda i, j: (r_blk + i, xg_blk + j)),
                  tile(), tile()] + _lru_weight_specs(lambda i, j: j),
        out_specs=(tile(), tile()),
        compiler_params=_params("parallel", "parallel"),
        name="lru_sample",
    )(proj, proj, prev_rows, h0_rows, *weights)


def _merge_kernel(o_ref, y_ref, wa_ref, wl_ref, ga_ref, gl_ref, ba_ref, bl_ref, out_ref):
    a_out = jnp.dot(o_ref[...], wa_ref[...], preferred_element_type=F32)
    r_out = jnp.dot(y_ref[...], wl_ref[...], preferred_element_type=F32)
    g_a = jax.nn.sigmoid(ga_ref[...] + ba_ref[...])
    g_l = jax.nn.sigmoid(gl_ref[...] + bl_ref[...])
    out_ref[...] = (g_a * a_out + g_l * r_out).astype(out_ref.dtype)


def _merge(o_attn, y_lru, w_attn_o, w_lru_o, proj, gate_col0, b_gate, *, tn=512):
    m, ka = o_attn.shape
    kl = y_lru.shape[1]
    d = w_attn_o.shape[1]
    g0 = gate_col0 // tn
    nd = d // tn
    return pl.pallas_call(
        _merge_kernel,
        out_shape=jax.ShapeDtypeStruct((m, d), BF16),
        grid=(m // ROW_TILE, nd),
        in_specs=[pl.BlockSpec((ROW_TILE, ka), lambda i, j: (i, 0)),
                  pl.BlockSpec((ROW_TILE, kl), lambda i, j: (i, 0)),
                  pl.BlockSpec((ka, tn), lambda i, j: (0, j)),
                  pl.BlockSpec((kl, tn), lambda i, j: (0, j)),
                  pl.BlockSpec((ROW_TILE, tn), lambda i, j: (i, g0 + j)),
                  pl.BlockSpec((ROW_TILE, tn), lambda i, j: (i, g0 + nd + j)),
                  pl.BlockSpec((1, tn), lambda i, j: (0, j)),
                  pl.BlockSpec((1, tn), lambda i, j: (0, nd + j))],
        out_specs=pl.BlockSpec((ROW_TILE, tn), lambda i, j: (i, j)),
        compiler_params=_params("parallel", "arbitrary"),
        name="merge",
    )(o_attn, y_lru, w_attn_o, w_lru_o, proj, proj, b_gate, b_gate)


def _residual_matmul_kernel(a_ref, w_ref, x_ref, o_ref):
    o_ref[...] = x_ref[...] + jnp.dot(a_ref[...], w_ref[...], preferred_element_type=F32)


def _residual_matmul(a, w, x, *, tm, tn, name):
    m, k = a.shape
    n = w.shape[1]
    return pl.pallas_call(
        _residual_matmul_kernel,
        out_shape=jax.ShapeDtypeStruct((m, n), F32),
        grid=(m // tm, n // tn),
        in_specs=[pl.BlockSpec((tm, k), lambda i, j: (i, 0)),
                  pl.BlockSpec((k, tn), lambda i, j: (0, j)),
                  pl.BlockSpec((tm, tn), lambda i, j: (i, j))],
        out_specs=pl.BlockSpec((tm, tn), lambda i, j: (i, j)),
        compiler_params=_params("parallel", "arbitrary"),
        name=name,
    )(a, w, x)


def _swiglu_up_kernel(x_ref, wg_ref, wu_ref, o_ref):
    x = x_ref[...]
    g = jnp.dot(x, wg_ref[...], preferred_element_type=F32)
    u = jnp.dot(x, wu_ref[...], preferred_element_type=F32)
    o_ref[...] = (jax.nn.silu(g) * u).astype(o_ref.dtype)


def _swiglu_up(xn, wg, wu, *, tn=256):
    m, d = xn.shape
    f = wg.shape[1]
    return pl.pallas_call(
        _swiglu_up_kernel,
        out_shape=jax.ShapeDtypeStruct((m, f), BF16),
        grid=(m // ROW_TILE, f // tn),
        in_specs=[pl.BlockSpec((ROW_TILE, d), lambda i, j: (i, 0)),
                  pl.BlockSpec((d, tn), lambda i, j: (0, j)),
                  pl.BlockSpec((d, tn), lambda i, j: (0, j))],
        out_specs=pl.BlockSpec((ROW_TILE, tn), lambda i, j: (i, j)),
        compiler_params=_params("parallel", "arbitrary"),
        name="swiglu_up",
    )(xn, wg, wu)


def kernel(x_prompt, x_sample, cache_k, cache_v, state_h, state_conv, page_table, norm_mix, w_in,
           b_gate, q_norm, k_norm, lambda_q1, lambda_k1, lambda_q2, lambda_k2, rel_bias,
           attn_subln, w_attn_o, conv_w, conv_b, w_r, b_r, w_i, b_i, lru_lambda, w_lru_o, w_out,
           norm_ffn, w_ffn_gate, w_ffn_up, w_ffn_down):
    bp, tp, d = x_prompt.shape
    bs, ts, _ = x_sample.shape
    depth = w_in.shape[0]
    assert depth == 1
    l = 0
    mp, ms = bp * tp, bs * ts
    qk_w = N_HEADS * HEAD_W
    lru_w = conv_b.shape[1]
    n_pages, page = page_table.shape[1], cache_k.shape[2]
    past = n_pages * page
    col_v, col_xr, col_xg, col_gate = 2 * qk_w, 3 * qk_w, 3 * qk_w + lru_w, 3 * qk_w + 2 * lru_w

    x_all = jnp.concatenate([x_prompt.reshape(mp, d), x_sample.reshape(ms, d)], axis=0)

    xn = _rmsnorm(x_all, norm_mix[l])
    qk_gain = jnp.concatenate([jnp.tile(q_norm[l], 2 * N_HEADS), jnp.tile(k_norm[l], 2 * N_HEADS)])
    proj = _in_proj(xn, w_in[l].astype(BF16), qk_gain.reshape(1, 2 * qk_w))

    r = jnp.arange(ATTN_TQ)[:, None]
    c = jnp.arange(ATTN_TK)[None, :]
    tiles = jnp.stack([_t5_bias(r - c, rel_bias), _t5_bias(ATTN_TK + r - c, rel_bias)])
    bias_tiles = jnp.transpose(tiles, (3, 0, 1, 2))
    bias_far = jnp.broadcast_to(_t5_bias(jnp.array(2 * ATTN_TK), rel_bias)[:, None, None],
                                (N_HEADS, 1, ATTN_TK))
    n_q = ts * N_HEADS
    row_tok = (jnp.arange(2 * n_q) // N_HEADS) % ts
    row_head = jnp.arange(2 * n_q) % N_HEADS

    def row_bias(dist):
        return jnp.take_along_axis(_t5_bias(dist, rel_bias), row_head[:, None, None], axis=2)[..., 0]

    def key_lanes(b, n_keys):
        lane_head = jnp.tile(jnp.arange(N_HEADS), n_keys)
        return jnp.where(lane_head[None, :] == row_head[:, None],
                         jnp.repeat(b, N_HEADS, axis=1), NEG_INF)

    b_past = key_lanes(row_bias(past + row_tok[:, None] - jnp.arange(past)[None, :]), past)
    bias_pages = b_past.reshape(2 * n_q, n_pages, page * N_HEADS).transpose(1, 0, 2)
    new_keys = jnp.arange(2 * ts)
    b_new = jnp.where(new_keys[None, :] <= row_tok[:, None],
                      row_bias(row_tok[:, None] - new_keys[None, :]), NEG_INF)
    bias_new = key_lanes(b_new, 2 * ts)

    lams = [v[l].reshape(1, D_HEAD) for v in (lambda_q1, lambda_k1, lambda_q2, lambda_k2)]
    subln = attn_subln[l]

    o_prompt = _prompt_attention(proj, bp, tp, bias_tiles, bias_far,
                                 subln.reshape(N_HEADS, 1, HEAD_W), lams)
    as_head_rows = lambda a: a.reshape(ms * N_HEADS, HEAD_W)
    o_sample = _sample_attention(as_head_rows(proj[mp:, :qk_w]), as_head_rows(proj[mp:, qk_w:2 * qk_w]),
                                 as_head_rows(proj[mp:, col_v:col_v + qk_w]), bs,
                                 cache_k, cache_v, l, page_table, bias_pages, bias_new,
                                 jnp.tile(subln, (ts, 1)), lams).reshape(ms, qk_w)

    lru_weights = (conv_w[l], conv_b[l].reshape(1, lru_w), w_r[l].astype(BF16),
                   b_r[l].reshape(1, lru_w), w_i[l].astype(BF16), b_i[l].reshape(1, lru_w),
                   lru_lambda[l].reshape(1, lru_w))
    y_prompt, h_prompt = _lru_prompt(proj, col_xr, col_xg, bp, tp, lru_weights)
    prev_rows = jnp.pad(state_conv[l], ((0, 0), (ts - (CONV_WIDTH - 1), 0), (0, 0))).reshape(ms, lru_w)
    h0_rows = jnp.repeat(state_h[l], ts, axis=0)
    y_sample, h_rows = _lru_sample(proj, mp, col_xr, col_xg, ms, ts, prev_rows, h0_rows, lru_weights)

    o_all = jnp.concatenate([o_prompt, o_sample], axis=0)
    y_all = jnp.concatenate([y_prompt, y_sample], axis=0)
    merged = _merge(o_all, y_all, w_attn_o[l].astype(BF16), w_lru_o[l].astype(BF16),
                    proj, col_gate, b_gate[l].reshape(1, 2 * d))
    x1 = _residual_matmul(merged, w_out[l].astype(BF16), x_all, tm=ROW_TILE, tn=512, name="out_proj")
    xn2 = _rmsnorm(x1, norm_ffn[l])
    hid = _swiglu_up(xn2, w_ffn_gate[l].astype(BF16), w_ffn_up[l].astype(BF16))
    x2 = _residual_matmul(hid, w_ffn_down[l].astype(BF16), x1, tm=512, tn=256, name="swiglu_down")

    k_all = proj[:, qk_w:2 * qk_w]
    v_all = proj[:, col_v:col_v + qk_w]
    xr_all = proj[:, col_xr:col_xr + lru_w]
    tail = CONV_WIDTH - 1
    return (x2[:mp].reshape(bp, tp, d), x2[mp:].reshape(bs, ts, d),
            k_all[:mp].reshape(1, bp, tp, N_HEADS, HEAD_W), v_all[:mp].reshape(1, bp, tp, N_HEADS, HEAD_W),
            h_prompt[None].astype(state_h.dtype),
            xr_all[:mp].reshape(bp, tp, lru_w)[None, :, tp - tail:, :].astype(state_conv.dtype),
            k_all[mp:].reshape(1, bs, ts, N_HEADS, HEAD_W), v_all[mp:].reshape(1, bs, ts, N_HEADS, HEAD_W),
            h_rows.reshape(bs, ts, lru_w)[None, :, ts - 1, :].astype(state_h.dtype),
            xr_all[mp:].reshape(bs, ts, lru_w)[None, :, ts - tail:, :].astype(state_conv.dtype))
```

```python
import functools
import math

import jax
import jax.numpy as jnp
import numpy as np
from jax import lax
from jax.experimental import pallas as pl
from jax.experimental.pallas import tpu as pltpu

F32 = jnp.float32
BF16 = jnp.bfloat16

N_HEADS = 8
D_HEAD = 128
HEAD_W = 2 * D_HEAD
LRU_BLOCK_DIM = 128
CONV_WIDTH = 4
LRU_C = 8.0
NUM_BUCKETS = 32
MAX_DISTANCE = 128
EPS = 1e-6
NEG_INF = -1e30
LAYER = 0
LAM_INIT = 0.8 - 0.6 * math.exp(-0.3 * LAYER)

VMEM_LIMIT_BYTES = 56 * 1024 * 1024
LANES = 128
SUBLANES = 8

ROW_TILE = 1024
NORM_ROWS = 256
ATTN_TQ = 256
ATTN_TK = 256
PAGES_PER_STEP = 4
LRU_ROWS = 256
LRU_LANES = 256


def _params(*semantics):
    return pltpu.CompilerParams(dimension_semantics=semantics,
                                vmem_limit_bytes=VMEM_LIMIT_BYTES)


def _lane_tile(x, width):
    return jnp.concatenate([x] * (width // LANES), axis=1)


def _rmsnorm_kernel(x_ref, g_ref, o_ref):
    x = x_ref[...]
    ms = jnp.mean(x * x, axis=-1, keepdims=True)
    o_ref[...] = (x * lax.rsqrt(ms + EPS) * g_ref[...]).astype(o_ref.dtype)


def _rmsnorm(x, g):
    m, d = x.shape
    return pl.pallas_call(
        _rmsnorm_kernel,
        out_shape=jax.ShapeDtypeStruct((m, d), BF16),
        grid=(m // NORM_ROWS,),
        in_specs=[pl.BlockSpec((NORM_ROWS, d), lambda i: (i, 0)),
                  pl.BlockSpec((1, d), lambda i: (0, 0))],
        out_specs=pl.BlockSpec((NORM_ROWS, d), lambda i: (i, 0)),
        compiler_params=_params("parallel"),
        name="rmsnorm",
    )(x, g.reshape(1, d))


def _in_proj_kernel(x_ref, w_ref, g_ref, o_ref, *, n_norm_tiles):
    j = pl.program_id(1)
    acc = jnp.dot(x_ref[...], w_ref[...].astype(BF16), preferred_element_type=F32)

    @pl.when(j < n_norm_tiles)
    def _():
        for c in range(0, acc.shape[1], D_HEAD):
            seg = acc[:, c:c + D_HEAD]
            ms = jnp.mean(seg * seg, axis=-1, keepdims=True)
            o_ref[:, c:c + D_HEAD] = seg * lax.rsqrt(ms + EPS) * g_ref[:, c:c + D_HEAD]

    @pl.when(j >= n_norm_tiles)
    def _():
        o_ref[...] = acc


def _in_proj(xn, w, qk_gain, *, tn=512):
    m, d = xn.shape
    n = w.shape[1]
    n_norm_tiles = qk_gain.shape[1] // tn
    return pl.pallas_call(
        functools.partial(_in_proj_kernel, n_norm_tiles=n_norm_tiles),
        out_shape=jax.ShapeDtypeStruct((m, n), F32),
        grid=(m // ROW_TILE, n // tn),
        in_specs=[pl.BlockSpec((ROW_TILE, d), lambda i, j: (i, 0)),
                  pl.BlockSpec((d, tn), lambda i, j: (0, j)),
                  pl.BlockSpec((1, tn), lambda i, j: (0, jnp.minimum(j, n_norm_tiles - 1)))],
        out_specs=pl.BlockSpec((ROW_TILE, tn), lambda i, j: (i, j)),
        compiler_params=_params("parallel", "arbitrary"),
        name="in_proj",
    )(xn, w, qk_gain)


def _bucket(dist):
    n = jnp.maximum(dist, 0)
    max_exact = NUM_BUCKETS // 2
    nf = jnp.maximum(n, max_exact).astype(F32)
    large = max_exact + (jnp.log(nf / max_exact) / math.log(MAX_DISTANCE / max_exact)
                         * (NUM_BUCKETS - max_exact)).astype(jnp.int32)
    return jnp.where(n < max_exact, n, jnp.minimum(large, NUM_BUCKETS - 1))


def _bucket_np(dist):
    n = np.maximum(dist, 0)
    max_exact = NUM_BUCKETS // 2
    nf = np.maximum(n, max_exact).astype(np.float32)
    large = max_exact + (np.log(nf / max_exact) / np.float32(math.log(MAX_DISTANCE / max_exact))
                         * (NUM_BUCKETS - max_exact)).astype(np.int32)
    return np.where(n < max_exact, n, np.minimum(large, NUM_BUCKETS - 1))


def _bias_tables_kernel(rb_ref, rbv_ref, ptile_ref, pfar_ref, spage_ref, snew_ref,
                        *, past, page, n_tok, pattern_pages):
    tq, tk = ptile_ref.shape[2], ptile_ref.shape[3]
    r = lax.broadcasted_iota(jnp.int32, (tq, tk), 0)
    c = lax.broadcasted_iota(jnp.int32, (tq, tk), 1)
    tile_buckets = (_bucket(r - c), _bucket(tk + r - c))
    far_bucket = _bucket(jnp.full((1, tk), 2 * tk, jnp.int32))

    def lookup_scalar(bucket, h):
        out = jnp.zeros(bucket.shape, F32)
        for b in range(NUM_BUCKETS):
            out = jnp.where(bucket == b, rb_ref[b, h], out)
        return out

    def per_head(h, carry):
        for t in range(2):
            ptile_ref[h, t] = lookup_scalar(tile_buckets[t], h)
        pfar_ref[h] = lookup_scalar(far_bucket, h)
        return carry
    lax.fori_loop(0, N_HEADS, per_head, 0)

    rows = snew_ref.shape[0]

    def lookup_rows(bucket):
        out = jnp.zeros(bucket.shape, F32)
        for b in range(NUM_BUCKETS):
            vals = _lane_tile(rbv_ref[b], bucket.shape[1])
            vals = jnp.concatenate([vals] * (bucket.shape[0] // SUBLANES), axis=0)
            out = jnp.where(bucket == b, vals, out)
        return out

    def sample_tile(width, dist_of):
        row = lax.broadcasted_iota(jnp.int32, (rows, width), 0)
        lane = lax.broadcasted_iota(jnp.int32, (rows, width), 1)
        tok = (row // N_HEADS) % n_tok
        key = lane // N_HEADS
        same_head = (row % N_HEADS) == (lane % N_HEADS)
        return tok, key, same_head, lookup_rows(_bucket(dist_of(tok, key)))

    for u, pg in enumerate(pattern_pages):
        tok, key, same_head, b = sample_tile(page * N_HEADS, lambda t, k: past + t - (pg * page + k))
        spage_ref[u] = jnp.where(same_head, b, NEG_INF)
    tok, key, same_head, b = sample_tile(snew_ref.shape[1], lambda t, k: t - k)
    snew_ref[...] = jnp.where(same_head & (key <= tok), b, NEG_INF)


def _bias_tables(rel_bias, past, page, n_tok, pattern_pages):
    rows = 2 * n_tok * N_HEADS
    rbv = jnp.broadcast_to(rel_bias[:, :, None], (NUM_BUCKETS, N_HEADS, LANES))
    return pl.pallas_call(
        functools.partial(_bias_tables_kernel, past=past, page=page, n_tok=n_tok,
                          pattern_pages=pattern_pages),
        out_shape=(jax.ShapeDtypeStruct((N_HEADS, 2, ATTN_TQ, ATTN_TK), F32),
                   jax.ShapeDtypeStruct((N_HEADS, 1, ATTN_TK), F32),
                   jax.ShapeDtypeStruct((len(pattern_pages), rows, page * N_HEADS), F32),
                   jax.ShapeDtypeStruct((rows, 2 * n_tok * N_HEADS), F32)),
        in_specs=[pl.BlockSpec(memory_space=pltpu.SMEM),
                  pl.BlockSpec(memory_space=pltpu.VMEM)],
        compiler_params=pltpu.CompilerParams(vmem_limit_bytes=VMEM_LIMIT_BYTES),
        name="bias_tables",
    )(rel_bias, rbv)


def _page_patterns(past, page, n_pages, n_tok):
    tok = np.arange(n_tok)[:, None]
    pos = np.arange(page)[None, :]
    keys = [_bucket_np(past + tok - (pg * page + pos)).tobytes() for pg in range(n_pages)]
    first = {}
    for pg, k in enumerate(keys):
        first.setdefault(k, pg)
    pattern_pages = tuple(sorted(first.values()))
    ids = np.array([pattern_pages.index(first[k]) for k in keys], np.int32)
    return pattern_pages, ids


def _diff_lambda(lq1, lk1, lq2, lk2):
    s1 = jnp.sum(lq1[...] * lk1[...], axis=-1, keepdims=True)
    s2 = jnp.sum(lq2[...] * lk2[...], axis=-1, keepdims=True)
    return jnp.exp(s1) - jnp.exp(s2) + LAM_INIT


def _head_subnorm(o, g):
    ms = jnp.mean(o * o, axis=-1, keepdims=True)
    return o * lax.rsqrt(ms + EPS) * g * (1.0 - LAM_INIT)


def _prompt_attn_kernel(q_ref, k_ref, v_ref, bias_ref, bfar_ref, g_ref, lq1, lk1, lq2, lk2,
                        o_ref, kb, vb):
    tq, tk = ATTN_TQ, ATTN_TK
    seq = q_ref.shape[0]
    scale = D_HEAD ** -0.5
    kb[...] = k_ref[...].astype(BF16)
    vb[...] = v_ref[...].astype(BF16)
    lam = _diff_lambda(lq1, lk1, lq2, lk2)
    row = lax.broadcasted_iota(jnp.int32, (tq, tk), 0)
    col = lax.broadcasted_iota(jnp.int32, (tq, tk), 1)
    causal = col <= row

    for i in range(seq // tq):
        n_keys = (i + 1) * tq
        n_chunks = n_keys // tk
        q = q_ref[i * tq:(i + 1) * tq, :]
        halves = []
        for off in (0, D_HEAD):
            s = lax.dot_general(q[:, off:off + D_HEAD].astype(BF16), kb[0:n_keys, off:off + D_HEAD],
                                (((1,), (1,)), ((), ())), preferred_element_type=F32) * scale
            chunks = []
            for c in range(n_chunks):
                sc = s[:, c * tk:(c + 1) * tk]
                if c == n_chunks - 1:
                    sc = jnp.where(causal, sc + bias_ref[0], NEG_INF)
                elif c == n_chunks - 2:
                    sc = sc + bias_ref[1]
                else:
                    sc = sc + bfar_ref[...]
                chunks.append(sc)
            m = chunks[0]
            for sc in chunks[1:]:
                m = jnp.maximum(m, sc)
            m = jnp.max(m, axis=-1, keepdims=True)
            p = [jnp.exp(sc - m) for sc in chunks]
            l = p[0]
            for pc in p[1:]:
                l = l + pc
            l = jnp.sum(l, axis=-1, keepdims=True)
            pv = jnp.dot(jnp.concatenate(p, axis=1).astype(BF16), vb[0:n_keys, :],
                         preferred_element_type=F32)
            halves.append(pv / l)
        o = halves[0] - lam * halves[1]
        o_ref[i * tq:(i + 1) * tq, :] = _head_subnorm(o, g_ref[...]).astype(o_ref.dtype)


def _prompt_attention(proj, n_batch, seq, bias_tiles, bias_far, subln, lams):
    k_col0 = N_HEADS
    v_col0 = 2 * N_HEADS
    vec = pl.BlockSpec((1, D_HEAD), lambda b, h: (0, 0))
    return pl.pallas_call(
        _prompt_attn_kernel,
        out_shape=jax.ShapeDtypeStruct((n_batch * seq, N_HEADS * HEAD_W), BF16),
        grid=(n_batch, N_HEADS),
        in_specs=[pl.BlockSpec((seq, HEAD_W), lambda b, h: (b, h)),
                  pl.BlockSpec((seq, HEAD_W), lambda b, h: (b, k_col0 + h)),
                  pl.BlockSpec((seq, HEAD_W), lambda b, h: (b, v_col0 + h)),
                  pl.BlockSpec((None, 2, ATTN_TQ, ATTN_TK), lambda b, h: (h, 0, 0, 0)),
                  pl.BlockSpec((None, 1, ATTN_TK), lambda b, h: (h, 0, 0)),
                  pl.BlockSpec((None, 1, HEAD_W), lambda b, h: (h, 0, 0)),
                  vec, vec, vec, vec],
        out_specs=pl.BlockSpec((seq, HEAD_W), lambda b, h: (b, h)),
        scratch_shapes=[pltpu.VMEM((seq, HEAD_W), BF16), pltpu.VMEM((seq, HEAD_W), BF16)],
        compiler_params=_params("parallel", "parallel"),
        name="prompt_attention",
    )(proj, proj, proj, bias_tiles, bias_far, subln, *lams)


def _sample_attn_kernel(pt_ref, pat_ref, q_ref, kn_ref, vn_ref, *rest):
    npg = PAGES_PER_STEP
    k_refs = rest[:npg]
    v_refs = rest[npg:2 * npg]
    (bias_ref, bias_new_ref, g_ref, lq1, lk1, lq2, lk2,
     o_ref, qmt, m_ref, l_ref, acc_ref) = rest[2 * npg:]
    del pt_ref
    p = pl.program_id(1)
    n_q = q_ref.shape[0]
    scale = D_HEAD ** -0.5

    def scores(k2d, bias):
        return lax.dot_general(qmt[...], k2d, (((1,), (1,)), ((), ())),
                               preferred_element_type=F32) * scale + bias

    def update(s_blocks, v_blocks):
        parts = []
        for s, v2d in zip(s_blocks, v_blocks):
            m_blk = jnp.max(s, axis=-1, keepdims=True)
            e = jnp.exp(s - m_blk)
            parts.append((m_blk, jnp.sum(e, axis=-1, keepdims=True),
                          jnp.dot(e.astype(BF16), v2d, preferred_element_type=F32)))
        m_old = m_ref[...]
        m_new = m_old
        for m_blk, _, _ in parts:
            m_new = jnp.maximum(m_new, m_blk)
        alpha = jnp.exp(m_old - m_new)
        l_new = alpha * l_ref[...]
        acc = _lane_tile(alpha, HEAD_W) * acc_ref[...]
        for m_blk, l_blk, pv in parts:
            w_blk = jnp.exp(m_blk - m_new)
            l_new = l_new + w_blk * l_blk
            acc = acc + _lane_tile(w_blk, HEAD_W) * pv
        m_ref[...] = m_new
        l_ref[...] = l_new
        acc_ref[...] = acc

    @pl.when(p == 0)
    def _():
        q = q_ref[...]
        lane = lax.broadcasted_iota(jnp.int32, q.shape, 1)
        qmt[...] = jnp.concatenate([jnp.where(lane < D_HEAD, q, 0.0),
                                    jnp.where(lane >= D_HEAD, q, 0.0)], axis=0).astype(BF16)
        m_ref[...] = jnp.full(m_ref.shape, NEG_INF, F32)
        l_ref[...] = jnp.zeros(l_ref.shape, F32)
        acc_ref[...] = jnp.zeros(acc_ref.shape, F32)
        pad = jnp.zeros((bias_new_ref.shape[1] - n_q, HEAD_W), F32)
        kn = jnp.concatenate([kn_ref[...], pad], axis=0).astype(BF16)
        vn = jnp.concatenate([vn_ref[...], pad], axis=0).astype(BF16)
        update([scores(kn, bias_new_ref[...])], [vn])

    s_blocks, v_blocks = [], []
    for i in range(npg):
        pos, heads, w = k_refs[i].shape
        k2d = k_refs[i][...].reshape(pos * heads, w).astype(BF16)
        s_blocks.append(scores(k2d, bias_ref[pat_ref[p * npg + i]]))
        v_blocks.append(v_refs[i][...].reshape(pos * heads, w).astype(BF16))
    update(s_blocks, v_blocks)

    @pl.when(p == pl.num_programs(1) - 1)
    def _():
        lam = _diff_lambda(lq1, lk1, lq2, lk2)
        inv_l = _lane_tile(1.0 / l_ref[...], HEAD_W)
        o_all = acc_ref[...] * inv_l
        o = o_all[0:n_q, :] - lam * o_all[n_q:2 * n_q, :]
        o_ref[...] = _head_subnorm(o, g_ref[...]).astype(o_ref.dtype)


def _sample_attention(q2d, kn2d, vn2d, n_seq, cache_k, cache_v, layer, page_table, page_pattern,
                      bias_pages, bias_new, subln_rows, lams):
    page, heads, w = cache_k.shape[2:]
    n_q = q2d.shape[0] // n_seq
    n_pages = page_table.shape[1]
    npg = PAGES_PER_STEP
    rows = 2 * n_q

    def page_spec(i):
        return pl.BlockSpec((None, None, page, heads, w),
                            lambda s, p, pt, pat: (layer, pt[s, p * npg + i], 0, 0, 0))

    seq_rows = pl.BlockSpec((n_q, w), lambda s, p, pt, pat: (s, 0))
    vec = pl.BlockSpec((1, D_HEAD), lambda s, p, pt, pat: (0, 0))
    in_specs = ([seq_rows, seq_rows, seq_rows]
                + [page_spec(i) for i in range(npg)]
                + [page_spec(i) for i in range(npg)]
                + [pl.BlockSpec(bias_pages.shape, lambda s, p, pt, pat: (0, 0, 0)),
                   pl.BlockSpec(bias_new.shape, lambda s, p, pt, pat: (0, 0)),
                   pl.BlockSpec((n_q, w), lambda s, p, pt, pat: (0, 0)),
                   vec, vec, vec, vec])
    return pl.pallas_call(
        _sample_attn_kernel,
        out_shape=jax.ShapeDtypeStruct((n_seq * n_q, w), BF16),
        grid_spec=pltpu.PrefetchScalarGridSpec(
            num_scalar_prefetch=2,
            grid=(n_seq, n_pages // npg),
            in_specs=in_specs,
            out_specs=seq_rows,
            scratch_shapes=[pltpu.VMEM((rows, w), BF16),
                            pltpu.VMEM((rows, LANES), F32), pltpu.VMEM((rows, LANES), F32),
                            pltpu.VMEM((rows, w), F32)]),
        compiler_params=_params("parallel", "arbitrary"),
        name="sample_attention",
    )(page_table, page_pattern, q2d, kn2d, vn2d, *([cache_k] * npg), *([cache_v] * npg),
      bias_pages, bias_new, subln_rows, *lams)


def _gelu_tanh(x):
    return 0.5 * x * (1.0 + jnp.tanh(math.sqrt(2.0 / math.pi) * (x + 0.044715 * (x * x * x))))


def _softplus(x):
    return jnp.maximum(x, 0.0) + jnp.log1p(jnp.exp(-jnp.abs(x)))


def _expm1(x):
    u = jnp.exp(x)
    um1 = u - 1.0
    return jnp.where(u == 1.0, x, jnp.where(um1 == -1.0, -1.0, um1 * x / jnp.log(u)))


def _lru_gates(xc, wr_ref, br_ref, wi_ref, bi_ref, lam_ref):
    r_parts, i_parts = [], []
    for n in range(xc.shape[1] // LRU_BLOCK_DIM):
        xb = xc[:, n * LRU_BLOCK_DIM:(n + 1) * LRU_BLOCK_DIM].astype(BF16)
        r_parts.append(jnp.dot(xb, wr_ref[n].astype(BF16), preferred_element_type=F32))
        i_parts.append(jnp.dot(xb, wi_ref[n].astype(BF16), preferred_element_type=F32))
    r = jax.nn.sigmoid(jnp.concatenate(r_parts, axis=1) + br_ref[...])
    i = jax.nn.sigmoid(jnp.concatenate(i_parts, axis=1) + bi_ref[...])
    log_a = -LRU_C * r * _softplus(-lam_ref[...])
    a = jnp.exp(log_a)
    u = jnp.sqrt(-_expm1(2.0 * log_a)) * i * xc
    return a, u


def _segmented_scan(a, u, tmod, seg_len):
    s = 1
    while s < seg_len:
        keep = tmod >= s
        a_prev = jnp.where(keep, pltpu.roll(a, s, 0), 1.0)
        u_prev = jnp.where(keep, pltpu.roll(u, s, 0), 0.0)
        u = a * u_prev + u
        a = a * a_prev
        s *= 2
    return a, u


def _conv_taps(x, tmod, prev_for_shift, cw_ref, cb_ref):
    xc = cb_ref[...] + cw_ref[CONV_WIDTH - 1:CONV_WIDTH, :] * x
    for k in range(1, CONV_WIDTH):
        shifted = jnp.where(tmod >= k, pltpu.roll(x, k, 0), prev_for_shift(k))
        xc = xc + cw_ref[CONV_WIDTH - 1 - k:CONV_WIDTH - k, :] * shifted
    return xc


def _lru_prompt_kernel(x_ref, xg_ref, cw_ref, cb_ref, wr_ref, br_ref, wi_ref, bi_ref, lam_ref,
                       y_ref, hlast_ref, prev_ref, h_ref):
    t = pl.program_id(2)
    rows, lanes = x_ref.shape

    @pl.when(t == 0)
    def _():
        prev_ref[...] = jnp.zeros(prev_ref.shape, F32)
        h_ref[...] = jnp.zeros(h_ref.shape, F32)

    x = x_ref[...]
    tmod = lax.broadcasted_iota(jnp.int32, (rows, lanes), 0)
    prev = prev_ref[...]

    def prev_for_shift(k):
        head = pltpu.roll(prev, k, 0)
        return jnp.concatenate([head, jnp.zeros((rows - SUBLANES, lanes), F32)], axis=0)

    xc = _conv_taps(x, tmod, prev_for_shift, cw_ref, cb_ref)
    a, u = _lru_gates(xc, wr_ref, br_ref, wi_ref, bi_ref, lam_ref)
    a_cum, b_cum = _segmented_scan(a, u, tmod, rows)
    h = a_cum * h_ref[0:1, :] + b_cum
    y_ref[...] = (h * _gelu_tanh(xg_ref[...])).astype(y_ref.dtype)
    h_last = h[rows - 1:rows, :]
    h_ref[...] = jnp.broadcast_to(h_last, h_ref.shape)
    hlast_ref[...] = h_last
    prev_ref[...] = x[rows - SUBLANES:rows, :]


def _lru_sample_kernel(x_ref, xg_ref, prev_ref, h0_ref, cw_ref, cb_ref, wr_ref, br_ref, wi_ref,
                       bi_ref, lam_ref, y_ref, h_out_ref, *, seg_len):
    rows, lanes = x_ref.shape
    x = x_ref[...]
    tmod = lax.broadcasted_iota(jnp.int32, (rows, lanes), 0) % seg_len
    prev = prev_ref[...]

    def prev_for_shift(k):
        return pltpu.roll(prev, rows + k - seg_len, 0)

    xc = _conv_taps(x, tmod, prev_for_shift, cw_ref, cb_ref)
    a, u = _lru_gates(xc, wr_ref, br_ref, wi_ref, bi_ref, lam_ref)
    a_cum, b_cum = _segmented_scan(a, u, tmod, seg_len)
    h = a_cum * h0_ref[...] + b_cum
    y_ref[...] = (h * _gelu_tanh(xg_ref[...])).astype(y_ref.dtype)
    h_out_ref[...] = h


def _lru_weight_specs(idx):
    nb = LRU_LANES // LRU_BLOCK_DIM
    vec = lambda rws: pl.BlockSpec((rws, LRU_LANES), lambda *g: (0, idx(*g)))
    mat = pl.BlockSpec((nb, LRU_BLOCK_DIM, LRU_BLOCK_DIM), lambda *g: (idx(*g), 0, 0))
    return [vec(CONV_WIDTH), vec(1), mat, vec(1), mat, vec(1), vec(1)]


def _lru_prompt(proj, xr_col0, xg_col0, n_batch, seq, weights):
    width = weights[1].shape[1]
    nt = seq // LRU_ROWS
    nl = width // LRU_LANES
    xr_blk, xg_blk = xr_col0 // LRU_LANES, xg_col0 // LRU_LANES
    y, h_last = pl.pallas_call(
        _lru_prompt_kernel,
        out_shape=(jax.ShapeDtypeStruct((n_batch * seq, width), BF16),
                   jax.ShapeDtypeStruct((n_batch, 1, width), F32)),
        grid=(n_batch, nl, nt),
        in_specs=[pl.BlockSpec((LRU_ROWS, LRU_LANES), lambda b, j, t: (b * nt + t, xr_blk + j)),
                  pl.BlockSpec((LRU_ROWS, LRU_LANES), lambda b, j, t: (b * nt + t, xg_blk + j))]
                 + _lru_weight_specs(lambda b, j, t: j),
        out_specs=(pl.BlockSpec((LRU_ROWS, LRU_LANES), lambda b, j, t: (b * nt + t, j)),
                   pl.BlockSpec((None, 1, LRU_LANES), lambda b, j, t: (b, 0, j))),
        scratch_shapes=[pltpu.VMEM((SUBLANES, LRU_LANES), F32), pltpu.VMEM((SUBLANES, LRU_LANES), F32)],
        compiler_params=_params("parallel", "parallel", "arbitrary"),
        name="lru_prompt",
    )(proj, proj, *weights)
    return y, h_last.reshape(n_batch, width)


def _lru_sample(proj, row0, xr_col0, xg_col0, n_rows, seg_len, prev_rows, h0_rows, weights):
    width = weights[1].shape[1]
    nr = n_rows // LRU_ROWS
    nl = width // LRU_LANES
    r_blk = row0 // LRU_ROWS
    xr_blk, xg_blk = xr_col0 // LRU_LANES, xg_col0 // LRU_LANES
    tile = lambda: pl.BlockSpec((LRU_ROWS, LRU_LANES), lambda i, j: (i, j))
    return pl.pallas_call(
        functools.partial(_lru_sample_kernel, seg_len=seg_len),
        out_shape=(jax.ShapeDtypeStruct((n_rows, width), BF16),
                   jax.ShapeDtypeStruct((n_rows, width), F32)),
        grid=(nr, nl),
        in_specs=[pl.BlockSpec((LRU_ROWS, LRU_LANES), lambda i, j: (r_blk + i, xr_blk + j)),
                  pl.BlockSpec((LRU_ROWS, LRU_LANES), lambda i, j: (r_blk + i, xg_blk + j)),
                  tile(), tile()] + _lru_weight_specs(lambda i, j: j),
        out_specs=(tile(), tile()),
        compiler_params=_params("parallel", "parallel"),
        name="lru_sample",
    )(proj, proj, prev_rows, h0_rows, *weights)


def _merge_kernel(o_ref, y_ref, wa_ref, wl_ref, ga_ref, gl_ref, ba_ref, bl_ref, out_ref):
    a_out = jnp.dot(o_ref[...], wa_ref[...].astype(BF16), preferred_element_type=F32)
    r_out = jnp.dot(y_ref[...], wl_ref[...].astype(BF16), preferred_element_type=F32)
    g_a = jax.nn.sigmoid(ga_ref[...] + ba_ref[...])
    g_l = jax.nn.sigmoid(gl_ref[...] + bl_ref[...])
    out_ref[...] = (g_a * a_out + g_l * r_out).astype(out_ref.dtype)


def _merge(o_attn, y_lru, w_attn_o, w_lru_o, proj, gate_col0, b_gate, *, tn=256):
    m, ka = o_attn.shape
    kl = y_lru.shape[1]
    d = w_attn_o.shape[1]
    g0 = gate_col0 // tn
    nd = d // tn
    return pl.pallas_call(
        _merge_kernel,
        out_shape=jax.ShapeDtypeStruct((m, d), BF16),
        grid=(m // ROW_TILE, nd),
        in_specs=[pl.BlockSpec((ROW_TILE, ka), lambda i, j: (i, 0)),
                  pl.BlockSpec((ROW_TILE, kl), lambda i, j: (i, 0)),
                  pl.BlockSpec((ka, tn), lambda i, j: (0, j)),
                  pl.BlockSpec((kl, tn), lambda i, j: (0, j)),
                  pl.BlockSpec((ROW_TILE, tn), lambda i, j: (i, g0 + j)),
                  pl.BlockSpec((ROW_TILE, tn), lambda i, j: (i, g0 + nd + j)),
                  pl.BlockSpec((1, tn), lambda i, j: (0, j)),
                  pl.BlockSpec((1, tn), lambda i, j: (0, nd + j))],
        out_specs=pl.BlockSpec((ROW_TILE, tn), lambda i, j: (i, j)),
        compiler_params=_params("parallel", "arbitrary"),
        name="merge",
    )(o_attn, y_lru, w_attn_o, w_lru_o, proj, proj, b_gate, b_gate)


def _residual_matmul_kernel(a_ref, w_ref, x_ref, o_ref):
    k = pl.program_id(2)
    part = jnp.dot(a_ref[...], w_ref[...].astype(BF16), preferred_element_type=F32)

    @pl.when(k == 0)
    def _():
        o_ref[...] = x_ref[...] + part

    @pl.when(k > 0)
    def _():
        o_ref[...] += part


def _residual_matmul(a, w, x, *, tn, k_splits, name):
    m, k = a.shape
    n = w.shape[1]
    tk = k // k_splits
    return pl.pallas_call(
        _residual_matmul_kernel,
        out_shape=jax.ShapeDtypeStruct((m, n), F32),
        grid=(m // ROW_TILE, n // tn, k_splits),
        in_specs=[pl.BlockSpec((ROW_TILE, tk), lambda i, j, s: (i, s)),
                  pl.BlockSpec((tk, tn), lambda i, j, s: (s, j)),
                  pl.BlockSpec((ROW_TILE, tn), lambda i, j, s: (i, j))],
        out_specs=pl.BlockSpec((ROW_TILE, tn), lambda i, j, s: (i, j)),
        compiler_params=_params("parallel", "arbitrary", "arbitrary"),
        name=name,
    )(a, w, x)


def _swiglu_up_kernel(x_ref, wg_ref, wu_ref, o_ref):
    x = x_ref[...]
    g = jnp.dot(x, wg_ref[...].astype(BF16), preferred_element_type=F32)
    u = jnp.dot(x, wu_ref[...].astype(BF16), preferred_element_type=F32)
    o_ref[...] = (jax.nn.silu(g) * u).astype(o_ref.dtype)


def _swiglu_up(xn, wg, wu, *, tn=256):
    m, d = xn.shape
    f = wg.shape[1]
    return pl.pallas_call(
        _swiglu_up_kernel,
        out_shape=jax.ShapeDtypeStruct((m, f), BF16),
        grid=(m // ROW_TILE, f // tn),
        in_specs=[pl.BlockSpec((ROW_TILE, d), lambda i, j: (i, 0)),
                  pl.BlockSpec((d, tn), lambda i, j: (0, j)),
                  pl.BlockSpec((d, tn), lambda i, j: (0, j))],
        out_specs=pl.BlockSpec((ROW_TILE, tn), lambda i, j: (i, j)),
        compiler_params=_params("parallel", "arbitrary"),
        name="swiglu_up",
    )(xn, wg, wu)


def kernel(x_prompt, x_sample, cache_k, cache_v, state_h, state_conv, page_table, norm_mix, w_in,
           b_gate, q_norm, k_norm, lambda_q1, lambda_k1, lambda_q2, lambda_k2, rel_bias,
           attn_subln, w_attn_o, conv_w, conv_b, w_r, b_r, w_i, b_i, lru_lambda, w_lru_o, w_out,
           norm_ffn, w_ffn_gate, w_ffn_up, w_ffn_down):
    bp, tp, d = x_prompt.shape
    bs, ts, _ = x_sample.shape
    depth = w_in.shape[0]
    assert depth == 1
    l = 0
    mp, ms = bp * tp, bs * ts
    qk_w = N_HEADS * HEAD_W
    lru_w = conv_b.shape[1]
    n_pages, page = page_table.shape[1], cache_k.shape[2]
    past = n_pages * page
    col_v, col_xr, col_xg, col_gate = 2 * qk_w, 3 * qk_w, 3 * qk_w + lru_w, 3 * qk_w + 2 * lru_w

    x_all = jnp.concatenate([x_prompt.reshape(mp, d), x_sample.reshape(ms, d)], axis=0)

    xn = _rmsnorm(x_all, norm_mix[l])
    qk_gain = jnp.concatenate([jnp.tile(q_norm[l], 2 * N_HEADS), jnp.tile(k_norm[l], 2 * N_HEADS)])
    proj = _in_proj(xn, w_in[l], qk_gain.reshape(1, 2 * qk_w))

    pattern_pages, page_pattern = _page_patterns(past, page, n_pages, ts)
    bias_tiles, bias_far, bias_pages, bias_new = _bias_tables(rel_bias, past, page, ts, pattern_pages)
    lams = [v[l].reshape(1, D_HEAD) for v in (lambda_q1, lambda_k1, lambda_q2, lambda_k2)]
    subln = attn_subln[l]
    o_prompt = _prompt_attention(proj, bp, tp, bias_tiles, bias_far,
                                 subln.reshape(N_HEADS, 1, HEAD_W), lams)
    as_head_rows = lambda a: a.reshape(ms * N_HEADS, HEAD_W)
    o_sample = _sample_attention(as_head_rows(proj[mp:, :qk_w]), as_head_rows(proj[mp:, qk_w:2 * qk_w]),
                                 as_head_rows(proj[mp:, col_v:col_v + qk_w]), bs,
                                 cache_k, cache_v, l, page_table, jnp.asarray(page_pattern),
                                 bias_pages, bias_new, jnp.tile(subln, (ts, 1)), lams).reshape(ms, qk_w)

    lru_weights = (conv_w[l], conv_b[l].reshape(1, lru_w), w_r[l], b_r[l].reshape(1, lru_w),
                   w_i[l], b_i[l].reshape(1, lru_w), lru_lambda[l].reshape(1, lru_w))
    y_prompt, h_prompt = _lru_prompt(proj, col_xr, col_xg, bp, tp, lru_weights)
    prev_rows = jnp.pad(state_conv[l], ((0, 0), (ts - (CONV_WIDTH - 1), 0), (0, 0))).reshape(ms, lru_w)
    h0_rows = jnp.repeat(state_h[l], ts, axis=0)
    y_sample, h_rows = _lru_sample(proj, mp, col_xr, col_xg, ms, ts, prev_rows, h0_rows, lru_weights)

    o_all = jnp.concatenate([o_prompt, o_sample], axis=0)
    y_all = jnp.concatenate([y_prompt, y_sample], axis=0)
    merged = _merge(o_all, y_all, w_attn_o[l], w_lru_o[l], proj, col_gate, b_gate[l].reshape(1, 2 * d))
    x1 = _residual_matmul(merged, w_out[l], x_all, tn=512, k_splits=1, name="out_proj")
    xn2 = _rmsnorm(x1, norm_ffn[l])
    hid = _swiglu_up(xn2, w_ffn_gate[l], w_ffn_up[l])
    x2 = _residual_matmul(hid, w_ffn_down[l], x1, tn=256, k_splits=2, name="swiglu_down")

    k_all = proj[:, qk_w:2 * qk_w]
    v_all = proj[:, col_v:col_v + qk_w]
    xr_all = proj[:, col_xr:col_xr + lru_w]
    tail = CONV_WIDTH - 1
    return (x2[:mp].reshape(bp, tp, d), x2[mp:].reshape(bs, ts, d),
            k_all[:mp].reshape(1, bp, tp, N_HEADS, HEAD_W), v_all[:mp].reshape(1, bp, tp, N_HEADS, HEAD_W),
            h_prompt[None].astype(state_h.dtype),
            xr_all[:mp].reshape(bp, tp, lru_w)[None, :, tp - tail:, :].astype(state_conv.dtype),
            k_all[mp:].reshape(1, bs, ts, N_HEADS, HEAD_W), v_all[mp:].reshape(1, bs, ts, N_HEADS, HEAD_W),
            h_rows.reshape(bs, ts, lru_w)[None, :, ts - 1, :].astype(state_h.dtype),
            xr_all[mp:].reshape(bs, ts, lru_w)[None, :, ts - tail:, :].astype(state_conv.dtype))
```

```python
import functools
import math

import jax
import jax.numpy as jnp
import numpy as np
from jax import lax
from jax.experimental import pallas as pl
from jax.experimental.pallas import tpu as pltpu

F32 = jnp.float32
BF16 = jnp.bfloat16

N_HEADS = 8
D_HEAD = 128
HEAD_W = 2 * D_HEAD
LRU_BLOCK_DIM = 128
CONV_WIDTH = 4
LRU_C = 8.0
NUM_BUCKETS = 32
MAX_DISTANCE = 128
EPS = 1e-6
NEG_INF = -1e30
LAYER = 0
LAM_INIT = 0.8 - 0.6 * math.exp(-0.3 * LAYER)

VMEM_LIMIT_BYTES = 56 * 1024 * 1024
LANES = 128
SUBLANES = 8

ROW_TILE = 1024
NORM_ROWS = 256
ATTN_TQ = 256
ATTN_TK = 256
PAGES_PER_STEP = 8
LRU_ROWS = 256
LRU_LANES = 256


def _params(*semantics):
    return pltpu.CompilerParams(dimension_semantics=semantics,
                                vmem_limit_bytes=VMEM_LIMIT_BYTES)


def _lane_tile(x, width):
    return jnp.concatenate([x] * (width // LANES), axis=1)


def _stacked_row_specs(parts, tm, cols, col_index, **spec_kwargs):
    specs, lo = [], 0
    for part in parts:
        n = part.shape[0] // tm
        specs.append(pl.BlockSpec(
            (tm, cols),
            lambda i, *g, lo=lo, n=n: (jnp.clip(i - lo, 0, n - 1), col_index(i, *g)),
            **spec_kwargs))
        lo += n
    return specs


def _part_tiles(parts, tm):
    return tuple(p.shape[0] // tm for p in parts)


def _select_part(i, refs, part_tiles):
    x = refs[-1][...]
    hi = sum(part_tiles[:-1])
    for ref, n in zip(reversed(refs[:-1]), reversed(part_tiles[:-1])):
        x = jnp.where(i < hi, ref[...], x)
        hi -= n
    return x


def _rmsnorm_kernel(*refs, part_tiles):
    *x_refs, g_ref, o_ref = refs
    x = _select_part(pl.program_id(0), x_refs, part_tiles)
    ms = jnp.mean(x * x, axis=-1, keepdims=True)
    o_ref[...] = (x * lax.rsqrt(ms + EPS) * g_ref[...]).astype(o_ref.dtype)


def _rmsnorm(parts, g):
    d = parts[0].shape[1]
    m = sum(p.shape[0] for p in parts)
    return pl.pallas_call(
        functools.partial(_rmsnorm_kernel, part_tiles=_part_tiles(parts, NORM_ROWS)),
        out_shape=jax.ShapeDtypeStruct((m, d), BF16),
        grid=(m // NORM_ROWS,),
        in_specs=_stacked_row_specs(parts, NORM_ROWS, d, lambda i: 0)
                 + [pl.BlockSpec((1, d), lambda i: (0, 0))],
        out_specs=pl.BlockSpec((NORM_ROWS, d), lambda i: (i, 0)),
        compiler_params=_params("parallel"),
        name="rmsnorm",
    )(*parts, g.reshape(1, d))


def _in_proj_kernel(x_ref, w_ref, g_ref, o_ref, *, n_norm_tiles):
    j = pl.program_id(1)
    acc = jnp.dot(x_ref[...], w_ref[...].astype(BF16), preferred_element_type=F32)

    @pl.when(j < n_norm_tiles)
    def _():
        for c in range(0, acc.shape[1], D_HEAD):
            seg = acc[:, c:c + D_HEAD]
            ms = jnp.mean(seg * seg, axis=-1, keepdims=True)
            o_ref[:, c:c + D_HEAD] = seg * lax.rsqrt(ms + EPS) * g_ref[:, c:c + D_HEAD]

    @pl.when(j >= n_norm_tiles)
    def _():
        o_ref[...] = acc


def _in_proj(xn, w, qk_gain, *, tn=512):
    m, d = xn.shape
    n = w.shape[1]
    n_norm_tiles = qk_gain.shape[1] // tn
    return pl.pallas_call(
        functools.partial(_in_proj_kernel, n_norm_tiles=n_norm_tiles),
        out_shape=jax.ShapeDtypeStruct((m, n), F32),
        grid=(m // ROW_TILE, n // tn),
        in_specs=[pl.BlockSpec((ROW_TILE, d), lambda i, j: (i, 0)),
                  pl.BlockSpec((d, tn), lambda i, j: (0, j)),
                  pl.BlockSpec((1, tn), lambda i, j: (0, jnp.minimum(j, n_norm_tiles - 1)))],
        out_specs=pl.BlockSpec((ROW_TILE, tn), lambda i, j: (i, j)),
        compiler_params=_params("parallel", "arbitrary"),
        name="in_proj",
    )(xn, w, qk_gain)


def _bucket(dist):
    n = jnp.maximum(dist, 0)
    max_exact = NUM_BUCKETS // 2
    nf = jnp.maximum(n, max_exact).astype(F32)
    large = max_exact + (jnp.log(nf / max_exact) / math.log(MAX_DISTANCE / max_exact)
                         * (NUM_BUCKETS - max_exact)).astype(jnp.int32)
    return jnp.where(n < max_exact, n, jnp.minimum(large, NUM_BUCKETS - 1))


def _bucket_np(dist):
    n = np.maximum(dist, 0)
    max_exact = NUM_BUCKETS // 2
    nf = np.maximum(n, max_exact).astype(np.float32)
    large = max_exact + (np.log(nf / max_exact) / np.float32(math.log(MAX_DISTANCE / max_exact))
                         * (NUM_BUCKETS - max_exact)).astype(np.int32)
    return np.where(n < max_exact, n, np.minimum(large, NUM_BUCKETS - 1))


def _bias_tables_kernel(rb_ref, rbv_ref, ptile_ref, pfar_ref, spage_ref, snew_ref,
                        *, past, page, n_tok, pattern_pages):
    tq, tk = ptile_ref.shape[2], ptile_ref.shape[3]
    r = lax.broadcasted_iota(jnp.int32, (tq, tk), 0)
    c = lax.broadcasted_iota(jnp.int32, (tq, tk), 1)
    tile_buckets = (_bucket(r - c), _bucket(tk + r - c))
    far_bucket = _bucket(jnp.full((1, tk), 2 * tk, jnp.int32))

    def lookup_scalar(bucket, h):
        out = jnp.zeros(bucket.shape, F32)
        for b in range(NUM_BUCKETS):
            out = jnp.where(bucket == b, rb_ref[b, h], out)
        return out

    def per_head(h, carry):
        for t in range(2):
            ptile_ref[h, t] = lookup_scalar(tile_buckets[t], h)
        pfar_ref[h] = lookup_scalar(far_bucket, h)
        return carry
    lax.fori_loop(0, N_HEADS, per_head, 0)

    rows = snew_ref.shape[0]

    def lookup_rows(bucket):
        out = jnp.zeros(bucket.shape, F32)
        for b in range(NUM_BUCKETS):
            vals = _lane_tile(rbv_ref[b], bucket.shape[1])
            vals = jnp.concatenate([vals] * (bucket.shape[0] // SUBLANES), axis=0)
            out = jnp.where(bucket == b, vals, out)
        return out

    def sample_tile(width, dist_of):
        row = lax.broadcasted_iota(jnp.int32, (rows, width), 0)
        lane = lax.broadcasted_iota(jnp.int32, (rows, width), 1)
        tok = (row // N_HEADS) % n_tok
        key = lane // N_HEADS
        same_head = (row % N_HEADS) == (lane % N_HEADS)
        return tok, key, same_head, lookup_rows(_bucket(dist_of(tok, key)))

    for u, pg in enumerate(pattern_pages):
        tok, key, same_head, b = sample_tile(page * N_HEADS, lambda t, k: past + t - (pg * page + k))
        spage_ref[u] = jnp.where(same_head, b, NEG_INF)
    tok, key, same_head, b = sample_tile(snew_ref.shape[1], lambda t, k: t - k)
    snew_ref[...] = jnp.where(same_head & (key <= tok), b, NEG_INF)


def _bias_tables(rel_bias, past, page, n_tok, pattern_pages):
    rows = 2 * n_tok * N_HEADS
    rbv = jnp.broadcast_to(rel_bias[:, :, None], (NUM_BUCKETS, N_HEADS, LANES))
    return pl.pallas_call(
        functools.partial(_bias_tables_kernel, past=past, page=page, n_tok=n_tok,
                          pattern_pages=pattern_pages),
        out_shape=(jax.ShapeDtypeStruct((N_HEADS, 2, ATTN_TQ, ATTN_TK), F32),
                   jax.ShapeDtypeStruct((N_HEADS, 1, ATTN_TK), F32),
                   jax.ShapeDtypeStruct((len(pattern_pages), rows, page * N_HEADS), F32),
                   jax.ShapeDtypeStruct((rows, 2 * n_tok * N_HEADS), F32)),
        in_specs=[pl.BlockSpec(memory_space=pltpu.SMEM),
                  pl.BlockSpec(memory_space=pltpu.VMEM)],
        compiler_params=pltpu.CompilerParams(vmem_limit_bytes=VMEM_LIMIT_BYTES),
        name="bias_tables",
    )(rel_bias, rbv)


def _page_patterns(past, page, n_pages, n_tok):
    tok = np.arange(n_tok)[:, None]
    pos = np.arange(page)[None, :]
    keys = [_bucket_np(past + tok - (pg * page + pos)).tobytes() for pg in range(n_pages)]
    first = {}
    for pg, k in enumerate(keys):
        first.setdefault(k, pg)
    pattern_pages = tuple(sorted(first.values()))
    ids = np.array([pattern_pages.index(first[k]) for k in keys], np.int32)
    return pattern_pages, ids


def _diff_lambda(lq1, lk1, lq2, lk2):
    s1 = jnp.sum(lq1[...] * lk1[...], axis=-1, keepdims=True)
    s2 = jnp.sum(lq2[...] * lk2[...], axis=-1, keepdims=True)
    return jnp.exp(s1) - jnp.exp(s2) + LAM_INIT


def _head_subnorm(o, g):
    ms = jnp.mean(o * o, axis=-1, keepdims=True)
    return o * lax.rsqrt(ms + EPS) * g * (1.0 - LAM_INIT)


def _prompt_attn_kernel(q_ref, k_ref, v_ref, bias_ref, bfar_ref, g_ref, lq1, lk1, lq2, lk2,
                        o_ref, kb, vb):
    tq, tk = ATTN_TQ, ATTN_TK
    seq = q_ref.shape[0]
    scale = D_HEAD ** -0.5
    kb[...] = k_ref[...].astype(BF16)
    vb[...] = v_ref[...].astype(BF16)
    lam = _diff_lambda(lq1, lk1, lq2, lk2)
    row = lax.broadcasted_iota(jnp.int32, (tq, tk), 0)
    col = lax.broadcasted_iota(jnp.int32, (tq, tk), 1)
    causal = col <= row

    for i in range(seq // tq):
        n_keys = (i + 1) * tq
        n_chunks = n_keys // tk
        q = q_ref[i * tq:(i + 1) * tq, :]
        halves = []
        for off in (0, D_HEAD):
            s = lax.dot_general(q[:, off:off + D_HEAD].astype(BF16), kb[0:n_keys, off:off + D_HEAD],
                                (((1,), (1,)), ((), ())), preferred_element_type=F32) * scale
            chunks = []
            for c in range(n_chunks):
                sc = s[:, c * tk:(c + 1) * tk]
                if c == n_chunks - 1:
                    sc = jnp.where(causal, sc + bias_ref[0], NEG_INF)
                elif c == n_chunks - 2:
                    sc = sc + bias_ref[1]
                else:
                    sc = sc + bfar_ref[...]
                chunks.append(sc)
            m = chunks[0]
            for sc in chunks[1:]:
                m = jnp.maximum(m, sc)
            m = jnp.max(m, axis=-1, keepdims=True)
            p = [jnp.exp(sc - m) for sc in chunks]
            l = p[0]
            for pc in p[1:]:
                l = l + pc
            l = jnp.sum(l, axis=-1, keepdims=True)
            pv = jnp.dot(jnp.concatenate(p, axis=1).astype(BF16), vb[0:n_keys, :],
                         preferred_element_type=F32)
            halves.append(pv / l)
        o = halves[0] - lam * halves[1]
        o_ref[i * tq:(i + 1) * tq, :] = _head_subnorm(o, g_ref[...]).astype(o_ref.dtype)


def _prompt_attention(proj, n_batch, seq, bias_tiles, bias_far, subln, lams):
    k_col0 = N_HEADS
    v_col0 = 2 * N_HEADS
    vec = pl.BlockSpec((1, D_HEAD), lambda b, h: (0, 0))
    return pl.pallas_call(
        _prompt_attn_kernel,
        out_shape=jax.ShapeDtypeStruct((n_batch * seq, N_HEADS * HEAD_W), BF16),
        grid=(n_batch, N_HEADS),
        in_specs=[pl.BlockSpec((seq, HEAD_W), lambda b, h: (b, h)),
                  pl.BlockSpec((seq, HEAD_W), lambda b, h: (b, k_col0 + h)),
                  pl.BlockSpec((seq, HEAD_W), lambda b, h: (b, v_col0 + h)),
                  pl.BlockSpec((None, 2, ATTN_TQ, ATTN_TK), lambda b, h: (h, 0, 0, 0)),
                  pl.BlockSpec((None, 1, ATTN_TK), lambda b, h: (h, 0, 0)),
                  pl.BlockSpec((None, 1, HEAD_W), lambda b, h: (h, 0, 0)),
                  vec, vec, vec, vec],
        out_specs=pl.BlockSpec((seq, HEAD_W), lambda b, h: (b, h)),
        scratch_shapes=[pltpu.VMEM((seq, HEAD_W), BF16), pltpu.VMEM((seq, HEAD_W), BF16)],
        compiler_params=_params("parallel", "parallel"),
        name="prompt_attention",
    )(proj, proj, proj, bias_tiles, bias_far, subln, *lams)


def _sample_attn_kernel(pt_ref, pat_ref, q_ref, kn_ref, vn_ref, *rest):
    npg = PAGES_PER_STEP
    k_refs = rest[:npg]
    v_refs = rest[npg:2 * npg]
    (bias_ref, bias_new_ref, g_ref, lq1, lk1, lq2, lk2,
     o_ref, qmt, m_ref, l_ref, acc_ref) = rest[2 * npg:]
    del pt_ref
    p = pl.program_id(1)
    n_q = q_ref.shape[0]
    scale = D_HEAD ** -0.5

    def scores(k2d, bias):
        return lax.dot_general(qmt[...], k2d, (((1,), (1,)), ((), ())),
                               preferred_element_type=F32) * scale + bias

    def update(s_blocks, v_blocks):
        parts = []
        for s, v2d in zip(s_blocks, v_blocks):
            m_blk = jnp.max(s, axis=-1, keepdims=True)
            e = jnp.exp(s - m_blk)
            parts.append((m_blk, jnp.sum(e, axis=-1, keepdims=True),
                          jnp.dot(e.astype(BF16), v2d, preferred_element_type=F32)))
        m_old = m_ref[...]
        m_new = m_old
        for m_blk, _, _ in parts:
            m_new = jnp.maximum(m_new, m_blk)
        alpha = jnp.exp(m_old - m_new)
        l_new = alpha * l_ref[...]
        acc = _lane_tile(alpha, HEAD_W) * acc_ref[...]
        for m_blk, l_blk, pv in parts:
            w_blk = jnp.exp(m_blk - m_new)
            l_new = l_new + w_blk * l_blk
            acc = acc + _lane_tile(w_blk, HEAD_W) * pv
        m_ref[...] = m_new
        l_ref[...] = l_new
        acc_ref[...] = acc

    @pl.when(p == 0)
    def _():
        q = q_ref[...]
        lane = lax.broadcasted_iota(jnp.int32, q.shape, 1)
        qmt[...] = jnp.concatenate([jnp.where(lane < D_HEAD, q, 0.0),
                                    jnp.where(lane >= D_HEAD, q, 0.0)], axis=0).astype(BF16)
        m_ref[...] = jnp.full(m_ref.shape, NEG_INF, F32)
        l_ref[...] = jnp.zeros(l_ref.shape, F32)
        acc_ref[...] = jnp.zeros(acc_ref.shape, F32)
        pad = jnp.zeros((bias_new_ref.shape[1] - n_q, HEAD_W), F32)
        kn = jnp.concatenate([kn_ref[...], pad], axis=0).astype(BF16)
        vn = jnp.concatenate([vn_ref[...], pad], axis=0).astype(BF16)
        update([scores(kn, bias_new_ref[...])], [vn])

    s_blocks, v_blocks = [], []
    for i in range(npg):
        pos, heads, w = k_refs[i].shape
        k2d = k_refs[i][...].reshape(pos * heads, w).astype(BF16)
        s_blocks.append(scores(k2d, bias_ref[pat_ref[p * npg + i]]))
        v_blocks.append(v_refs[i][...].reshape(pos * heads, w).astype(BF16))
    update(s_blocks, v_blocks)

    @pl.when(p == pl.num_programs(1) - 1)
    def _():
        lam = _diff_lambda(lq1, lk1, lq2, lk2)
        inv_l = _lane_tile(1.0 / l_ref[...], HEAD_W)
        o_all = acc_ref[...] * inv_l
        o = o_all[0:n_q, :] - lam * o_all[n_q:2 * n_q, :]
        o_ref[...] = _head_subnorm(o, g_ref[...]).astype(o_ref.dtype)


def _sample_attention(q2d, kn2d, vn2d, n_seq, cache_k, cache_v, layer, page_table, page_pattern,
                      bias_pages, bias_new, subln_rows, lams):
    page, heads, w = cache_k.shape[2:]
    n_q = q2d.shape[0] // n_seq
    n_pages = page_table.shape[1]
    npg = PAGES_PER_STEP
    rows = 2 * n_q

    def page_spec(i):
        return pl.BlockSpec((None, None, page, heads, w),
                            lambda s, p, pt, pat: (layer, pt[s, p * npg + i], 0, 0, 0))

    seq_rows = pl.BlockSpec((n_q, w), lambda s, p, pt, pat: (s, 0))
    vec = pl.BlockSpec((1, D_HEAD), lambda s, p, pt, pat: (0, 0))
    in_specs = ([seq_rows, seq_rows, seq_rows]
                + [page_spec(i) for i in range(npg)]
                + [page_spec(i) for i in range(npg)]
                + [pl.BlockSpec(bias_pages.shape, lambda s, p, pt, pat: (0, 0, 0)),
                   pl.BlockSpec(bias_new.shape, lambda s, p, pt, pat: (0, 0)),
                   pl.BlockSpec((n_q, w), lambda s, p, pt, pat: (0, 0)),
                   vec, vec, vec, vec])
    return pl.pallas_call(
        _sample_attn_kernel,
        out_shape=jax.ShapeDtypeStruct((n_seq * n_q, w), BF16),
        grid_spec=pltpu.PrefetchScalarGridSpec(
            num_scalar_prefetch=2,
            grid=(n_seq, n_pages // npg),
            in_specs=in_specs,
            out_specs=seq_rows,
            scratch_shapes=[pltpu.VMEM((rows, w), BF16),
                            pltpu.VMEM((rows, LANES), F32), pltpu.VMEM((rows, LANES), F32),
                            pltpu.VMEM((rows, w), F32)]),
        compiler_params=_params("parallel", "arbitrary"),
        name="sample_attention",
    )(page_table, page_pattern, q2d, kn2d, vn2d, *([cache_k] * npg), *([cache_v] * npg),
      bias_pages, bias_new, subln_rows, *lams)


def _gelu_tanh(x):
    return 0.5 * x * (1.0 + jnp.tanh(math.sqrt(2.0 / math.pi) * (x + 0.044715 * (x * x * x))))


def _softplus(x):
    return jnp.maximum(x, 0.0) + jnp.log1p(jnp.exp(-jnp.abs(x)))


def _expm1(x):
    u = jnp.exp(x)
    um1 = u - 1.0
    return jnp.where(u == 1.0, x, jnp.where(um1 == -1.0, -1.0, um1 * x / jnp.log(u)))


def _lru_gates(xc, wr_ref, br_ref, wi_ref, bi_ref, lam_ref):
    r_parts, i_parts = [], []
    for n in range(xc.shape[1] // LRU_BLOCK_DIM):
        xb = xc[:, n * LRU_BLOCK_DIM:(n + 1) * LRU_BLOCK_DIM].astype(BF16)
        r_parts.append(jnp.dot(xb, wr_ref[n].astype(BF16), preferred_element_type=F32))
        i_parts.append(jnp.dot(xb, wi_ref[n].astype(BF16), preferred_element_type=F32))
    r = jax.nn.sigmoid(jnp.concatenate(r_parts, axis=1) + br_ref[...])
    i = jax.nn.sigmoid(jnp.concatenate(i_parts, axis=1) + bi_ref[...])
    log_a = -LRU_C * r * _softplus(-lam_ref[...])
    a = jnp.exp(log_a)
    u = jnp.sqrt(-_expm1(2.0 * log_a)) * i * xc
    return a, u


def _segmented_scan(a, u, tmod, seg_len):
    s = 1
    while s < seg_len:
        keep = tmod >= s
        a_prev = jnp.where(keep, pltpu.roll(a, s, 0), 1.0)
        u_prev = jnp.where(keep, pltpu.roll(u, s, 0), 0.0)
        u = a * u_prev + u
        a = a * a_prev
        s *= 2
    return a, u


def _conv_taps(x, tmod, prev_for_shift, cw_ref, cb_ref):
    xc = cb_ref[...] + cw_ref[CONV_WIDTH - 1:CONV_WIDTH, :] * x
    for k in range(1, CONV_WIDTH):
        shifted = jnp.where(tmod >= k, pltpu.roll(x, k, 0), prev_for_shift(k))
        xc = xc + cw_ref[CONV_WIDTH - 1 - k:CONV_WIDTH - k, :] * shifted
    return xc


def _lru_prompt_kernel(x_ref, xg_ref, cw_ref, cb_ref, wr_ref, br_ref, wi_ref, bi_ref, lam_ref,
                       y_ref, hlast_ref, prev_ref, h_ref):
    t = pl.program_id(2)
    rows, lanes = x_ref.shape

    @pl.when(t == 0)
    def _():
        prev_ref[...] = jnp.zeros(prev_ref.shape, F32)
        h_ref[...] = jnp.zeros(h_ref.shape, F32)

    x = x_ref[...]
    tmod = lax.broadcasted_iota(jnp.int32, (rows, lanes), 0)
    prev = prev_ref[...]

    def prev_for_shift(k):
        head = pltpu.roll(prev, k, 0)
        return jnp.concatenate([head, jnp.zeros((rows - SUBLANES, lanes), F32)], axis=0)

    xc = _conv_taps(x, tmod, prev_for_shift, cw_ref, cb_ref)
    a, u = _lru_gates(xc, wr_ref, br_ref, wi_ref, bi_ref, lam_ref)
    a_cum, b_cum = _segmented_scan(a, u, tmod, rows)
    h = a_cum * h_ref[0:1, :] + b_cum
    y_ref[...] = (h * _gelu_tanh(xg_ref[...])).astype(y_ref.dtype)
    h_last = h[rows - 1:rows, :]
    h_ref[...] = jnp.broadcast_to(h_last, h_ref.shape)
    hlast_ref[...] = h_last
    prev_ref[...] = x[rows - SUBLANES:rows, :]


def _lru_sample_kernel(x_ref, xg_ref, prev_ref, h0_ref, cw_ref, cb_ref, wr_ref, br_ref, wi_ref,
                       bi_ref, lam_ref, y_ref, h_out_ref, *, seg_len):
    rows, lanes = x_ref.shape
    x = x_ref[...]
    tmod = lax.broadcasted_iota(jnp.int32, (rows, lanes), 0) % seg_len
    prev = prev_ref[...]

    def prev_for_shift(k):
        return pltpu.roll(prev, rows + k - seg_len, 0)

    xc = _conv_taps(x, tmod, prev_for_shift, cw_ref, cb_ref)
    a, u = _lru_gates(xc, wr_ref, br_ref, wi_ref, bi_ref, lam_ref)
    a_cum, b_cum = _segmented_scan(a, u, tmod, seg_len)
    h = a_cum * h0_ref[...] + b_cum
    y_ref[...] = (h * _gelu_tanh(xg_ref[...])).astype(y_ref.dtype)
    h_out_ref[...] = h


def _lru_weight_specs(idx):
    nb = LRU_LANES // LRU_BLOCK_DIM
    vec = lambda rws: pl.BlockSpec((rws, LRU_LANES), lambda *g: (0, idx(*g)))
    mat = pl.BlockSpec((nb, LRU_BLOCK_DIM, LRU_BLOCK_DIM), lambda *g: (idx(*g), 0, 0))
    return [vec(CONV_WIDTH), vec(1), mat, vec(1), mat, vec(1), vec(1)]


def _lru_prompt(proj, xr_col0, xg_col0, n_batch, seq, weights):
    width = weights[1].shape[1]
    nt = seq // LRU_ROWS
    nl = width // LRU_LANES
    xr_blk, xg_blk = xr_col0 // LRU_LANES, xg_col0 // LRU_LANES
    y, h_last = pl.pallas_call(
        _lru_prompt_kernel,
        out_shape=(jax.ShapeDtypeStruct((n_batch * seq, width), BF16),
                   jax.ShapeDtypeStruct((n_batch, 1, width), F32)),
        grid=(n_batch, nl, nt),
        in_specs=[pl.BlockSpec((LRU_ROWS, LRU_LANES), lambda b, j, t: (b * nt + t, xr_blk + j)),
                  pl.BlockSpec((LRU_ROWS, LRU_LANES), lambda b, j, t: (b * nt + t, xg_blk + j))]
                 + _lru_weight_specs(lambda b, j, t: j),
        out_specs=(pl.BlockSpec((LRU_ROWS, LRU_LANES), lambda b, j, t: (b * nt + t, j)),
                   pl.BlockSpec((None, 1, LRU_LANES), lambda b, j, t: (b, 0, j))),
        scratch_shapes=[pltpu.VMEM((SUBLANES, LRU_LANES), F32), pltpu.VMEM((SUBLANES, LRU_LANES), F32)],
        compiler_params=_params("parallel", "parallel", "arbitrary"),
        name="lru_prompt",
    )(proj, proj, *weights)
    return y, h_last.reshape(n_batch, width)


def _lru_sample(proj, row0, xr_col0, xg_col0, n_rows, seg_len, prev_rows, h0_rows, weights):
    width = weights[1].shape[1]
    nr = n_rows // LRU_ROWS
    nl = width // LRU_LANES
    r_blk = row0 // LRU_ROWS
    xr_blk, xg_blk = xr_col0 // LRU_LANES, xg_col0 // LRU_LANES
    tile = lambda: pl.BlockSpec((LRU_ROWS, LRU_LANES), lambda i, j: (i, j))
    return pl.pallas_call(
        functools.partial(_lru_sample_kernel, seg_len=seg_len),
        out_shape=(jax.ShapeDtypeStruct((n_rows, width), BF16),
                   jax.ShapeDtypeStruct((n_rows, width), F32)),
        grid=(nr, nl),
        in_specs=[pl.BlockSpec((LRU_ROWS, LRU_LANES), lambda i, j: (r_blk + i, xr_blk + j)),
                  pl.BlockSpec((LRU_ROWS, LRU_LANES), lambda i, j: (r_blk + i, xg_blk + j)),
                  tile(), tile()] + _lru_weight_specs(lambda i, j: j),
        out_specs=(tile(), tile()),
        compiler_params=_params("parallel", "parallel"),
        name="lru_sample",
    )(proj, proj, prev_rows, h0_rows, *weights)


def _merge_kernel(*refs, o_tiles, y_tiles):
    o_refs = refs[:len(o_tiles)]
    y_refs = refs[len(o_tiles):len(o_tiles) + len(y_tiles)]
    wa_ref, wl_ref, ga_ref, gl_ref, ba_ref, bl_ref, out_ref = refs[len(o_tiles) + len(y_tiles):]
    i = pl.program_id(0)
    a_out = jnp.dot(_select_part(i, o_refs, o_tiles), wa_ref[...].astype(BF16),
                    preferred_element_type=F32)
    r_out = jnp.dot(_select_part(i, y_refs, y_tiles), wl_ref[...].astype(BF16),
                    preferred_element_type=F32)
    g_a = jax.nn.sigmoid(ga_ref[...] + ba_ref[...])
    g_l = jax.nn.sigmoid(gl_ref[...] + bl_ref[...])
    out_ref[...] = (g_a * a_out + g_l * r_out).astype(out_ref.dtype)


def _merge(o_parts, y_parts, w_attn_o, w_lru_o, proj, gate_col0, b_gate, *, tn=512):
    ka, kl = o_parts[0].shape[1], y_parts[0].shape[1]
    m = sum(p.shape[0] for p in o_parts)
    d = w_attn_o.shape[1]
    g0 = gate_col0 // tn
    nd = d // tn
    once = dict(pipeline_mode=pl.Buffered(1))
    return pl.pallas_call(
        functools.partial(_merge_kernel, o_tiles=_part_tiles(o_parts, ROW_TILE),
                          y_tiles=_part_tiles(y_parts, ROW_TILE)),
        out_shape=jax.ShapeDtypeStruct((m, d), BF16),
        grid=(m // ROW_TILE, nd),
        in_specs=_stacked_row_specs(o_parts, ROW_TILE, ka, lambda i, j: 0, **once)
                 + _stacked_row_specs(y_parts, ROW_TILE, kl, lambda i, j: 0, **once)
                 + [pl.BlockSpec((ka, tn), lambda i, j: (0, j)),
                    pl.BlockSpec((kl, tn), lambda i, j: (0, j)),
                    pl.BlockSpec((ROW_TILE, tn), lambda i, j: (i, g0 + j)),
                    pl.BlockSpec((ROW_TILE, tn), lambda i, j: (i, g0 + nd + j)),
                    pl.BlockSpec((1, tn), lambda i, j: (0, j)),
                    pl.BlockSpec((1, tn), lambda i, j: (0, nd + j))],
        out_specs=pl.BlockSpec((ROW_TILE, tn), lambda i, j: (i, j)),
        compiler_params=_params("arbitrary", "arbitrary"),
        name="merge",
    )(*o_parts, *y_parts, w_attn_o, w_lru_o, proj, proj, b_gate, b_gate)


def _out_proj_kernel(*refs, x_tiles):
    a_ref, w_ref, *x_refs, o_ref = refs
    x = _select_part(pl.program_id(0), x_refs, x_tiles)
    o_ref[...] = x + jnp.dot(a_ref[...], w_ref[...].astype(BF16), preferred_element_type=F32)


def _out_proj(a, w, x_parts, *, tn=512):
    m, k = a.shape
    n = w.shape[1]
    return pl.pallas_call(
        functools.partial(_out_proj_kernel, x_tiles=_part_tiles(x_parts, ROW_TILE)),
        out_shape=jax.ShapeDtypeStruct((m, n), F32),
        grid=(m // ROW_TILE, n // tn),
        in_specs=[pl.BlockSpec((ROW_TILE, k), lambda i, j: (i, 0)),
                  pl.BlockSpec((k, tn), lambda i, j: (0, j))]
                 + _stacked_row_specs(x_parts, ROW_TILE, tn, lambda i, j: j),
        out_specs=pl.BlockSpec((ROW_TILE, tn), lambda i, j: (i, j)),
        compiler_params=_params("arbitrary", "arbitrary"),
        name="out_proj",
    )(a, w, *x_parts)


def _swiglu_down_kernel(a0_ref, a1_ref, w_ref, x_ref, *o_refs, o_tiles):
    i, k = pl.program_id(0), pl.program_id(2)
    w = w_ref[...].astype(BF16)

    def emit(a_ref, first):
        part = jnp.dot(a_ref[...], w, preferred_element_type=F32)
        lo = 0
        for o_ref, n in zip(o_refs, o_tiles):
            @pl.when((i >= lo) & (i < lo + n))
            def _():
                if first:
                    o_ref[...] = x_ref[...] + part
                else:
                    o_ref[...] += part
            lo += n

    @pl.when(k == 0)
    def _():
        emit(a0_ref, True)

    @pl.when(k == 1)
    def _():
        emit(a1_ref, False)


def _swiglu_down(a, w, x, out_rows, *, tn=256):
    m, k = a.shape
    n = w.shape[1]
    tk = k // 2
    nj = n // tn
    o_tiles = tuple(r // ROW_TILE for r in out_rows)
    once = dict(pipeline_mode=pl.Buffered(1))

    def out_spec(lo, cnt):
        def index(i, j, s):
            mine = (i >= lo) & (i < lo + cnt)
            return (jnp.clip(i - lo, 0, cnt - 1), jnp.where(mine, j, jnp.where(i < lo, 0, nj - 1)))
        return pl.BlockSpec((ROW_TILE, tn), index)

    out_specs, lo = [], 0
    for cnt in o_tiles:
        out_specs.append(out_spec(lo, cnt))
        lo += cnt
    return pl.pallas_call(
        functools.partial(_swiglu_down_kernel, o_tiles=o_tiles),
        out_shape=tuple(jax.ShapeDtypeStruct((r, n), F32) for r in out_rows),
        grid=(m // ROW_TILE, nj, 2),
        in_specs=[pl.BlockSpec((ROW_TILE, tk), lambda i, j, s: (i, 0), **once),
                  pl.BlockSpec((ROW_TILE, tk), lambda i, j, s: (i, 1), **once),
                  pl.BlockSpec((tk, tn), lambda i, j, s: (s, j)),
                  pl.BlockSpec((ROW_TILE, tn), lambda i, j, s: (i, j))],
        out_specs=tuple(out_specs),
        compiler_params=_params("arbitrary", "arbitrary", "arbitrary"),
        name="swiglu_down",
    )(a, a, w, x)


def _swiglu_up_kernel(x_ref, wg_ref, wu_ref, o_ref):
    x = x_ref[...]
    g = jnp.dot(x, wg_ref[...].astype(BF16), preferred_element_type=F32)
    u = jnp.dot(x, wu_ref[...].astype(BF16), preferred_element_type=F32)
    o_ref[...] = (jax.nn.silu(g) * u).astype(o_ref.dtype)


def _swiglu_up(xn, wg, wu, *, tn=256):
    m, d = xn.shape
    f = wg.shape[1]
    return pl.pallas_call(
        _swiglu_up_kernel,
        out_shape=jax.ShapeDtypeStruct((m, f), BF16),
        grid=(m // ROW_TILE, f // tn),
        in_specs=[pl.BlockSpec((ROW_TILE, d), lambda i, j: (i, 0)),
                  pl.BlockSpec((d, tn), lambda i, j: (0, j)),
                  pl.BlockSpec((d, tn), lambda i, j: (0, j))],
        out_specs=pl.BlockSpec((ROW_TILE, tn), lambda i, j: (i, j)),
        compiler_params=_params("parallel", "arbitrary"),
        name="swiglu_up",
    )(xn, wg, wu)


def kernel(x_prompt, x_sample, cache_k, cache_v, state_h, state_conv, page_table, norm_mix, w_in,
           b_gate, q_norm, k_norm, lambda_q1, lambda_k1, lambda_q2, lambda_k2, rel_bias,
           attn_subln, w_attn_o, conv_w, conv_b, w_r, b_r, w_i, b_i, lru_lambda, w_lru_o, w_out,
           norm_ffn, w_ffn_gate, w_ffn_up, w_ffn_down):
    bp, tp, d = x_prompt.shape
    bs, ts, _ = x_sample.shape
    depth = w_in.shape[0]
    assert depth == 1
    l = 0
    mp, ms = bp * tp, bs * ts
    qk_w = N_HEADS * HEAD_W
    lru_w = conv_b.shape[1]
    n_pages, page = page_table.shape[1], cache_k.shape[2]
    past = n_pages * page
    col_v, col_xr, col_xg, col_gate = 2 * qk_w, 3 * qk_w, 3 * qk_w + lru_w, 3 * qk_w + 2 * lru_w

    x_parts = (x_prompt.reshape(mp, d), x_sample.reshape(ms, d))

    xn = _rmsnorm(x_parts, norm_mix[l])
    qk_gain = jnp.concatenate([jnp.tile(q_norm[l], 2 * N_HEADS), jnp.tile(k_norm[l], 2 * N_HEADS)])
    proj = _in_proj(xn, w_in[l], qk_gain.reshape(1, 2 * qk_w))

    pattern_pages, page_pattern = _page_patterns(past, page, n_pages, ts)
    bias_tiles, bias_far, bias_pages, bias_new = _bias_tables(rel_bias, past, page, ts, pattern_pages)
    lams = [v[l].reshape(1, D_HEAD) for v in (lambda_q1, lambda_k1, lambda_q2, lambda_k2)]
    subln = attn_subln[l]
    o_prompt = _prompt_attention(proj, bp, tp, bias_tiles, bias_far,
                                 subln.reshape(N_HEADS, 1, HEAD_W), lams)
    as_head_rows = lambda a: a.reshape(ms * N_HEADS, HEAD_W)
    o_sample = _sample_attention(as_head_rows(proj[mp:, :qk_w]), as_head_rows(proj[mp:, qk_w:2 * qk_w]),
                                 as_head_rows(proj[mp:, col_v:col_v + qk_w]), bs,
                                 cache_k, cache_v, l, page_table, jnp.asarray(page_pattern),
                                 bias_pages, bias_new, jnp.tile(subln, (ts, 1)), lams).reshape(ms, qk_w)

    lru_weights = (conv_w[l], conv_b[l].reshape(1, lru_w), w_r[l], b_r[l].reshape(1, lru_w),
                   w_i[l], b_i[l].reshape(1, lru_w), lru_lambda[l].reshape(1, lru_w))
    y_prompt, h_prompt = _lru_prompt(proj, col_xr, col_xg, bp, tp, lru_weights)
    prev_rows = jnp.pad(state_conv[l], ((0, 0), (ts - (CONV_WIDTH - 1), 0), (0, 0))).reshape(ms, lru_w)
    h0_rows = jnp.repeat(state_h[l], ts, axis=0)
    y_sample, h_rows = _lru_sample(proj, mp, col_xr, col_xg, ms, ts, prev_rows, h0_rows, lru_weights)

    merged = _merge((o_prompt, o_sample), (y_prompt, y_sample), w_attn_o[l], w_lru_o[l],
                    proj, col_gate, b_gate[l].reshape(1, 2 * d))
    x1 = _out_proj(merged, w_out[l], x_parts)
    xn2 = _rmsnorm((x1,), norm_ffn[l])
    hid = _swiglu_up(xn2, w_ffn_gate[l], w_ffn_up[l])
    out_prompt, out_sample = _swiglu_down(hid, w_ffn_down[l], x1, (mp, ms))

    k_all = proj[:, qk_w:2 * qk_w]
    v_all = proj[:, col_v:col_v + qk_w]
    xr_all = proj[:, col_xr:col_xr + lru_w]
    tail = CONV_WIDTH - 1
    return (out_prompt.reshape(bp, tp, d), out_sample.reshape(bs, ts, d),
            k_all[:mp].reshape(1, bp, tp, N_HEADS, HEAD_W), v_all[:mp].reshape(1, bp, tp, N_HEADS, HEAD_W),
            h_prompt[None].astype(state_h.dtype),
            xr_all[:mp].reshape(bp, tp, lru_w)[None, :, tp - tail:, :].astype(state_conv.dtype),
            k_all[mp:].reshape(1, bs, ts, N_HEADS, HEAD_W), v_all[mp:].reshape(1, bs, ts, N_HEADS, HEAD_W),
            h_rows.reshape(bs, ts, lru_w)[None, :, ts - 1, :].astype(state_h.dtype),
            xr_all[mp:].reshape(bs, ts, lru_w)[None, :, ts - tail:, :].astype(state_conv.dtype))
```

```python
import functools
import math

import jax
import jax.numpy as jnp
import numpy as np
from jax import lax
from jax.experimental import pallas as pl
from jax.experimental.pallas import tpu as pltpu

F32 = jnp.float32
BF16 = jnp.bfloat16

N_HEADS = 8
D_HEAD = 128
HEAD_W = 2 * D_HEAD
LRU_BLOCK_DIM = 128
CONV_WIDTH = 4
LRU_C = 8.0
NUM_BUCKETS = 32
MAX_DISTANCE = 128
EPS = 1e-6
NEG_INF = -1e30
LAYER = 0
LAM_INIT = 0.8 - 0.6 * math.exp(-0.3 * LAYER)

VMEM_LIMIT_BYTES = 56 * 1024 * 1024
LANES = 128
SUBLANES = 8

ROW_TILE = 1024
NORM_ROWS = 256
ATTN_TQ = 256
ATTN_TK = 256
PAGES_PER_STEP = 8
LRU_ROWS = 256
LRU_LANES = 512


def _params(*semantics):
    return pltpu.CompilerParams(dimension_semantics=semantics,
                                vmem_limit_bytes=VMEM_LIMIT_BYTES)


def _lane_tile(x, width):
    return jnp.concatenate([x] * (width // LANES), axis=1)


def _stacked_row_specs(parts, tm, cols, col_index, **spec_kwargs):
    specs, lo = [], 0
    for part in parts:
        n = part.shape[0] // tm
        specs.append(pl.BlockSpec(
            (tm, cols),
            lambda i, *g, lo=lo, n=n: (jnp.clip(i - lo, 0, n - 1), col_index(i, *g)),
            **spec_kwargs))
        lo += n
    return specs


def _part_tiles(parts, tm):
    return tuple(p.shape[0] // tm for p in parts)


def _select_part(i, refs, part_tiles):
    x = refs[-1][...]
    hi = sum(part_tiles[:-1])
    for ref, n in zip(reversed(refs[:-1]), reversed(part_tiles[:-1])):
        x = jnp.where(i < hi, ref[...], x)
        hi -= n
    return x


def _rmsnorm_kernel(*refs, part_tiles):
    *x_refs, g_ref, o_ref = refs
    x = _select_part(pl.program_id(0), x_refs, part_tiles)
    ms = jnp.mean(x * x, axis=-1, keepdims=True)
    o_ref[...] = (x * lax.rsqrt(ms + EPS) * g_ref[...]).astype(o_ref.dtype)


def _rmsnorm(parts, g):
    d = parts[0].shape[1]
    m = sum(p.shape[0] for p in parts)
    return pl.pallas_call(
        functools.partial(_rmsnorm_kernel, part_tiles=_part_tiles(parts, NORM_ROWS)),
        out_shape=jax.ShapeDtypeStruct((m, d), BF16),
        grid=(m // NORM_ROWS,),
        in_specs=_stacked_row_specs(parts, NORM_ROWS, d, lambda i: 0)
                 + [pl.BlockSpec((1, d), lambda i: (0, 0))],
        out_specs=pl.BlockSpec((NORM_ROWS, d), lambda i: (i, 0)),
        compiler_params=_params("parallel"),
        name="rmsnorm",
    )(*parts, g.reshape(1, d))


def _in_proj_kernel(x_ref, w_ref, g_ref, o_ref, *, n_norm_tiles):
    j = pl.program_id(1)
    acc = jnp.dot(x_ref[...], w_ref[...].astype(BF16), preferred_element_type=F32)

    @pl.when(j < n_norm_tiles)
    def _():
        for c in range(0, acc.shape[1], D_HEAD):
            seg = acc[:, c:c + D_HEAD]
            ms = jnp.mean(seg * seg, axis=-1, keepdims=True)
            o_ref[:, c:c + D_HEAD] = seg * lax.rsqrt(ms + EPS) * g_ref[:, c:c + D_HEAD]

    @pl.when(j >= n_norm_tiles)
    def _():
        o_ref[...] = acc


def _in_proj(xn, w, qk_gain, *, tn=512):
    m, d = xn.shape
    n = w.shape[1]
    n_norm_tiles = qk_gain.shape[1] // tn
    return pl.pallas_call(
        functools.partial(_in_proj_kernel, n_norm_tiles=n_norm_tiles),
        out_shape=jax.ShapeDtypeStruct((m, n), F32),
        grid=(m // ROW_TILE, n // tn),
        in_specs=[pl.BlockSpec((ROW_TILE, d), lambda i, j: (i, 0)),
                  pl.BlockSpec((d, tn), lambda i, j: (0, j)),
                  pl.BlockSpec((1, tn), lambda i, j: (0, jnp.minimum(j, n_norm_tiles - 1)))],
        out_specs=pl.BlockSpec((ROW_TILE, tn), lambda i, j: (i, j)),
        compiler_params=_params("parallel", "arbitrary"),
        name="in_proj",
    )(xn, w, qk_gain)


def _bucket(dist):
    n = jnp.maximum(dist, 0)
    max_exact = NUM_BUCKETS // 2
    nf = jnp.maximum(n, max_exact).astype(F32)
    large = max_exact + (jnp.log(nf / max_exact) / math.log(MAX_DISTANCE / max_exact)
                         * (NUM_BUCKETS - max_exact)).astype(jnp.int32)
    return jnp.where(n < max_exact, n, jnp.minimum(large, NUM_BUCKETS - 1))


def _bucket_np(dist):
    n = np.maximum(dist, 0)
    max_exact = NUM_BUCKETS // 2
    nf = np.maximum(n, max_exact).astype(np.float32)
    large = max_exact + (np.log(nf / max_exact) / np.float32(math.log(MAX_DISTANCE / max_exact))
                         * (NUM_BUCKETS - max_exact)).astype(np.int32)
    return np.where(n < max_exact, n, np.minimum(large, NUM_BUCKETS - 1))


def _bias_tables_kernel(rb_ref, rbv_ref, ptile_ref, pfar_ref, spage_ref, snew_ref,
                        *, past, page, n_tok, pattern_pages):
    tq, tk = ptile_ref.shape[2], ptile_ref.shape[3]
    r = lax.broadcasted_iota(jnp.int32, (tq, tk), 0)
    c = lax.broadcasted_iota(jnp.int32, (tq, tk), 1)
    tile_buckets = (_bucket(r - c), _bucket(tk + r - c))
    far_bucket = _bucket(jnp.full((1, tk), 2 * tk, jnp.int32))

    def lookup_scalar(bucket, h):
        out = jnp.zeros(bucket.shape, F32)
        for b in range(NUM_BUCKETS):
            out = jnp.where(bucket == b, rb_ref[b, h], out)
        return out

    def per_head(h, carry):
        for t in range(2):
            ptile_ref[h, t] = lookup_scalar(tile_buckets[t], h)
        pfar_ref[h] = lookup_scalar(far_bucket, h)
        return carry
    lax.fori_loop(0, N_HEADS, per_head, 0)

    rows = snew_ref.shape[0]

    def lookup_rows(bucket):
        out = jnp.zeros(bucket.shape, F32)
        for b in range(NUM_BUCKETS):
            vals = _lane_tile(rbv_ref[b], bucket.shape[1])
            vals = jnp.concatenate([vals] * (bucket.shape[0] // SUBLANES), axis=0)
            out = jnp.where(bucket == b, vals, out)
        return out

    def sample_tile(width, dist_of):
        row = lax.broadcasted_iota(jnp.int32, (rows, width), 0)
        lane = lax.broadcasted_iota(jnp.int32, (rows, width), 1)
        tok = (row // N_HEADS) % n_tok
        key = lane // N_HEADS
        same_head = (row % N_HEADS) == (lane % N_HEADS)
        return tok, key, same_head, lookup_rows(_bucket(dist_of(tok, key)))

    for u, pg in enumerate(pattern_pages):
        tok, key, same_head, b = sample_tile(page * N_HEADS, lambda t, k: past + t - (pg * page + k))
        spage_ref[u] = jnp.where(same_head, b, NEG_INF)
    tok, key, same_head, b = sample_tile(snew_ref.shape[1], lambda t, k: t - k)
    snew_ref[...] = jnp.where(same_head & (key <= tok), b, NEG_INF)


def _bias_tables(rel_bias, past, page, n_tok, pattern_pages):
    rows = 2 * n_tok * N_HEADS
    rbv = jnp.broadcast_to(rel_bias[:, :, None], (NUM_BUCKETS, N_HEADS, LANES))
    return pl.pallas_call(
        functools.partial(_bias_tables_kernel, past=past, page=page, n_tok=n_tok,
                          pattern_pages=pattern_pages),
        out_shape=(jax.ShapeDtypeStruct((N_HEADS, 2, ATTN_TQ, ATTN_TK), F32),
                   jax.ShapeDtypeStruct((N_HEADS, 1, ATTN_TK), F32),
                   jax.ShapeDtypeStruct((len(pattern_pages), rows, page * N_HEADS), F32),
                   jax.ShapeDtypeStruct((rows, 2 * n_tok * N_HEADS), F32)),
        in_specs=[pl.BlockSpec(memory_space=pltpu.SMEM),
                  pl.BlockSpec(memory_space=pltpu.VMEM)],
        compiler_params=pltpu.CompilerParams(vmem_limit_bytes=VMEM_LIMIT_BYTES),
        name="bias_tables",
    )(rel_bias, rbv)


def _page_patterns(past, page, n_pages, n_tok):
    tok = np.arange(n_tok)[:, None]
    pos = np.arange(page)[None, :]
    keys = [_bucket_np(past + tok - (pg * page + pos)).tobytes() for pg in range(n_pages)]
    first = {}
    for pg, k in enumerate(keys):
        first.setdefault(k, pg)
    pattern_pages = tuple(sorted(first.values()))
    ids = np.array([pattern_pages.index(first[k]) for k in keys], np.int32)
    return pattern_pages, ids


def _diff_lambda(lq1, lk1, lq2, lk2):
    s1 = jnp.sum(lq1[...] * lk1[...], axis=-1, keepdims=True)
    s2 = jnp.sum(lq2[...] * lk2[...], axis=-1, keepdims=True)
    return jnp.exp(s1) - jnp.exp(s2) + LAM_INIT


def _head_subnorm(o, g):
    ms = jnp.mean(o * o, axis=-1, keepdims=True)
    return o * lax.rsqrt(ms + EPS) * g * (1.0 - LAM_INIT)


def _prompt_attn_kernel(q_ref, k_ref, v_ref, bias_ref, bfar_ref, g_ref, lq1, lk1, lq2, lk2,
                        o_ref, kb, vb):
    tq, tk = ATTN_TQ, ATTN_TK
    seq = q_ref.shape[0]
    scale = D_HEAD ** -0.5
    kb[...] = k_ref[...].astype(BF16)
    vb[...] = v_ref[...].astype(BF16)
    lam = _diff_lambda(lq1, lk1, lq2, lk2)
    row = lax.broadcasted_iota(jnp.int32, (tq, tk), 0)
    col = lax.broadcasted_iota(jnp.int32, (tq, tk), 1)
    causal = col <= row

    for i in range(seq // tq):
        n_keys = (i + 1) * tq
        n_chunks = n_keys // tk
        q = q_ref[i * tq:(i + 1) * tq, :]
        halves = []
        for off in (0, D_HEAD):
            s = lax.dot_general(q[:, off:off + D_HEAD].astype(BF16), kb[0:n_keys, off:off + D_HEAD],
                                (((1,), (1,)), ((), ())), preferred_element_type=F32) * scale
            chunks = []
            for c in range(n_chunks):
                sc = s[:, c * tk:(c + 1) * tk]
                if c == n_chunks - 1:
                    sc = jnp.where(causal, sc + bias_ref[0], NEG_INF)
                elif c == n_chunks - 2:
                    sc = sc + bias_ref[1]
                else:
                    sc = sc + bfar_ref[...]
                chunks.append(sc)
            m = chunks[0]
            for sc in chunks[1:]:
                m = jnp.maximum(m, sc)
            m = jnp.max(m, axis=-1, keepdims=True)
            p = [jnp.exp(sc - m) for sc in chunks]
            l = p[0]
            for pc in p[1:]:
                l = l + pc
            l = jnp.sum(l, axis=-1, keepdims=True)
            pv = jnp.dot(jnp.concatenate(p, axis=1).astype(BF16), vb[0:n_keys, :],
                         preferred_element_type=F32)
            halves.append(pv * (1.0 / l))
        o = halves[0] - lam * halves[1]
        o_ref[i * tq:(i + 1) * tq, :] = _head_subnorm(o, g_ref[...]).astype(o_ref.dtype)


def _prompt_attention(proj, n_batch, seq, bias_tiles, bias_far, subln, lams):
    k_col0 = N_HEADS
    v_col0 = 2 * N_HEADS
    vec = pl.BlockSpec((1, D_HEAD), lambda b, h: (0, 0))
    return pl.pallas_call(
        _prompt_attn_kernel,
        out_shape=jax.ShapeDtypeStruct((n_batch * seq, N_HEADS * HEAD_W), BF16),
        grid=(n_batch, N_HEADS),
        in_specs=[pl.BlockSpec((seq, HEAD_W), lambda b, h: (b, h)),
                  pl.BlockSpec((seq, HEAD_W), lambda b, h: (b, k_col0 + h)),
                  pl.BlockSpec((seq, HEAD_W), lambda b, h: (b, v_col0 + h)),
                  pl.BlockSpec((None, 2, ATTN_TQ, ATTN_TK), lambda b, h: (h, 0, 0, 0)),
                  pl.BlockSpec((None, 1, ATTN_TK), lambda b, h: (h, 0, 0)),
                  pl.BlockSpec((None, 1, HEAD_W), lambda b, h: (h, 0, 0)),
                  vec, vec, vec, vec],
        out_specs=pl.BlockSpec((seq, HEAD_W), lambda b, h: (b, h)),
        scratch_shapes=[pltpu.VMEM((seq, HEAD_W), BF16), pltpu.VMEM((seq, HEAD_W), BF16)],
        compiler_params=_params("parallel", "parallel"),
        name="prompt_attention",
    )(proj, proj, proj, bias_tiles, bias_far, subln, *lams)


def _sample_attn_kernel(pt_ref, pat_ref, q_ref, kn_ref, vn_ref, *rest):
    npg = PAGES_PER_STEP
    k_refs = rest[:npg]
    v_refs = rest[npg:2 * npg]
    (bias_ref, bias_new_ref, g_ref, lq1, lk1, lq2, lk2,
     o_ref, qmt, m_ref, l_ref, acc_ref) = rest[2 * npg:]
    del pt_ref
    p = pl.program_id(1)
    n_q = q_ref.shape[0]
    scale = D_HEAD ** -0.5

    def scores(k2d, bias):
        return lax.dot_general(qmt[...], k2d, (((1,), (1,)), ((), ())),
                               preferred_element_type=F32) * scale + bias

    def update(s_blocks, v_blocks):
        parts = []
        for s, v2d in zip(s_blocks, v_blocks):
            m_blk = jnp.max(s, axis=-1, keepdims=True)
            e = jnp.exp(s - m_blk)
            parts.append((m_blk, jnp.sum(e, axis=-1, keepdims=True),
                          jnp.dot(e.astype(BF16), v2d, preferred_element_type=F32)))
        m_old = m_ref[...]
        m_new = m_old
        for m_blk, _, _ in parts:
            m_new = jnp.maximum(m_new, m_blk)
        alpha = jnp.exp(m_old - m_new)
        l_new = alpha * l_ref[...]
        acc = _lane_tile(alpha, HEAD_W) * acc_ref[...]
        for m_blk, l_blk, pv in parts:
            w_blk = jnp.exp(m_blk - m_new)
            l_new = l_new + w_blk * l_blk
            acc = acc + _lane_tile(w_blk, HEAD_W) * pv
        m_ref[...] = m_new
        l_ref[...] = l_new
        acc_ref[...] = acc

    @pl.when(p == 0)
    def _():
        q = q_ref[...]
        lane = lax.broadcasted_iota(jnp.int32, q.shape, 1)
        qmt[...] = jnp.concatenate([jnp.where(lane < D_HEAD, q, 0.0),
                                    jnp.where(lane >= D_HEAD, q, 0.0)], axis=0).astype(BF16)
        m_ref[...] = jnp.full(m_ref.shape, NEG_INF, F32)
        l_ref[...] = jnp.zeros(l_ref.shape, F32)
        acc_ref[...] = jnp.zeros(acc_ref.shape, F32)
        pad = jnp.zeros((bias_new_ref.shape[1] - n_q, HEAD_W), F32)
        kn = jnp.concatenate([kn_ref[...], pad], axis=0).astype(BF16)
        vn = jnp.concatenate([vn_ref[...], pad], axis=0).astype(BF16)
        update([scores(kn, bias_new_ref[...])], [vn])

    s_blocks, v_blocks = [], []
    for i in range(npg):
        pos, heads, w = k_refs[i].shape
        k2d = k_refs[i][...].reshape(pos * heads, w).astype(BF16)
        s_blocks.append(scores(k2d, bias_ref[pat_ref[p * npg + i]]))
        v_blocks.append(v_refs[i][...].reshape(pos * heads, w).astype(BF16))
    update(s_blocks, v_blocks)

    @pl.when(p == pl.num_programs(1) - 1)
    def _():
        lam = _diff_lambda(lq1, lk1, lq2, lk2)
        inv_l = _lane_tile(1.0 / l_ref[...], HEAD_W)
        o_all = acc_ref[...] * inv_l
        o = o_all[0:n_q, :] - lam * o_all[n_q:2 * n_q, :]
        o_ref[...] = _head_subnorm(o, g_ref[...]).astype(o_ref.dtype)


def _sample_attention(q2d, kn2d, vn2d, n_seq, cache_k, cache_v, layer, page_table, page_pattern,
                      bias_pages, bias_new, subln_rows, lams):
    page, heads, w = cache_k.shape[2:]
    n_q = q2d.shape[0] // n_seq
    n_pages = page_table.shape[1]
    npg = PAGES_PER_STEP
    rows = 2 * n_q

    def page_spec(i):
        return pl.BlockSpec((None, None, page, heads, w),
                            lambda s, p, pt, pat: (layer, pt[s, p * npg + i], 0, 0, 0))

    seq_rows = pl.BlockSpec((n_q, w), lambda s, p, pt, pat: (s, 0))
    vec = pl.BlockSpec((1, D_HEAD), lambda s, p, pt, pat: (0, 0))
    in_specs = ([seq_rows, seq_rows, seq_rows]
                + [page_spec(i) for i in range(npg)]
                + [page_spec(i) for i in range(npg)]
                + [pl.BlockSpec(bias_pages.shape, lambda s, p, pt, pat: (0, 0, 0)),
                   pl.BlockSpec(bias_new.shape, lambda s, p, pt, pat: (0, 0)),
                   pl.BlockSpec((n_q, w), lambda s, p, pt, pat: (0, 0)),
                   vec, vec, vec, vec])
    return pl.pallas_call(
        _sample_attn_kernel,
        out_shape=jax.ShapeDtypeStruct((n_seq * n_q, w), BF16),
        grid_spec=pltpu.PrefetchScalarGridSpec(
            num_scalar_prefetch=2,
            grid=(n_seq, n_pages // npg),
            in_specs=in_specs,
            out_specs=seq_rows,
            scratch_shapes=[pltpu.VMEM((rows, w), BF16),
                            pltpu.VMEM((rows, LANES), F32), pltpu.VMEM((rows, LANES), F32),
                            pltpu.VMEM((rows, w), F32)]),
        compiler_params=_params("parallel", "arbitrary"),
        name="sample_attention",
    )(page_table, page_pattern, q2d, kn2d, vn2d, *([cache_k] * npg), *([cache_v] * npg),
      bias_pages, bias_new, subln_rows, *lams)


def _gelu_tanh(x):
    return 0.5 * x * (1.0 + jnp.tanh(math.sqrt(2.0 / math.pi) * (x + 0.044715 * (x * x * x))))


def _softplus(x):
    return jnp.maximum(x, 0.0) + jnp.log1p(jnp.exp(-jnp.abs(x)))


def _expm1(x):
    u = jnp.exp(x)
    um1 = u - 1.0
    return jnp.where(u == 1.0, x, jnp.where(um1 == -1.0, -1.0, um1 * x * (1.0 / jnp.log(u))))


def _lru_gates(xc, wr_ref, br_ref, wi_ref, bi_ref, lam_ref):
    r_parts, i_parts = [], []
    for n in range(xc.shape[1] // LRU_BLOCK_DIM):
        xb = xc[:, n * LRU_BLOCK_DIM:(n + 1) * LRU_BLOCK_DIM].astype(BF16)
        r_parts.append(jnp.dot(xb, wr_ref[n].astype(BF16), preferred_element_type=F32))
        i_parts.append(jnp.dot(xb, wi_ref[n].astype(BF16), preferred_element_type=F32))
    r = jax.nn.sigmoid(jnp.concatenate(r_parts, axis=1) + br_ref[...])
    i = jax.nn.sigmoid(jnp.concatenate(i_parts, axis=1) + bi_ref[...])
    log_a = -LRU_C * r * _softplus(-lam_ref[...])
    a = jnp.exp(log_a)
    u = jnp.sqrt(-_expm1(2.0 * log_a)) * i * xc
    return a, u


def _segmented_scan(a, u, tmod, seg_len):
    s = 1
    while s < seg_len:
        keep = tmod >= s
        a_prev = jnp.where(keep, pltpu.roll(a, s, 0), 1.0)
        u_prev = jnp.where(keep, pltpu.roll(u, s, 0), 0.0)
        u = a * u_prev + u
        a = a * a_prev
        s *= 2
    return a, u


def _chained_scan(a, u, tmod, h_in):
    a_loc, b_loc = _segmented_scan(a, u, tmod % SUBLANES, SUBLANES)
    groups, carry = [], h_in
    for g in range(a.shape[0] // SUBLANES):
        rows = slice(g * SUBLANES, (g + 1) * SUBLANES)
        h_g = a_loc[rows] * carry + b_loc[rows]
        carry = h_g[SUBLANES - 1:SUBLANES]
        groups.append(h_g)
    return jnp.concatenate(groups, axis=0)


def _conv_taps(x, tmod, prev_for_shift, cw_ref, cb_ref):
    xc = cb_ref[...] + cw_ref[CONV_WIDTH - 1:CONV_WIDTH, :] * x
    for k in range(1, CONV_WIDTH):
        shifted = jnp.where(tmod >= k, pltpu.roll(x, k, 0), prev_for_shift(k))
        xc = xc + cw_ref[CONV_WIDTH - 1 - k:CONV_WIDTH - k, :] * shifted
    return xc


def _lru_prompt_kernel(x_ref, xg_ref, cw_ref, cb_ref, wr_ref, br_ref, wi_ref, bi_ref, lam_ref,
                       y_ref, hlast_ref, xpad_ref, h_ref):
    t = pl.program_id(2)
    rows, lanes = x_ref.shape

    @pl.when(t == 0)
    def _():
        xpad_ref[0:SUBLANES, :] = jnp.zeros((SUBLANES, lanes), F32)
        h_ref[...] = jnp.zeros(h_ref.shape, F32)

    x = x_ref[...]
    xpad_ref[SUBLANES:SUBLANES + rows, :] = x
    xc = cb_ref[...] + cw_ref[CONV_WIDTH - 1:CONV_WIDTH, :] * x
    for k in range(1, CONV_WIDTH):
        xc = xc + (cw_ref[CONV_WIDTH - 1 - k:CONV_WIDTH - k, :]
                   * xpad_ref[SUBLANES - k:SUBLANES - k + rows, :])
    tmod = lax.broadcasted_iota(jnp.int32, (rows, lanes), 0)
    a, u = _lru_gates(xc, wr_ref, br_ref, wi_ref, bi_ref, lam_ref)
    h = _chained_scan(a, u, tmod, h_ref[0:1, :])
    y_ref[...] = (h * _gelu_tanh(xg_ref[...])).astype(y_ref.dtype)
    h_last = h[rows - 1:rows, :]
    h_ref[...] = jnp.broadcast_to(h_last, h_ref.shape)
    hlast_ref[...] = h_last
    xpad_ref[0:SUBLANES, :] = x[rows - SUBLANES:rows, :]


def _lru_sample_kernel(x_ref, xg_ref, prev_ref, h0_ref, cw_ref, cb_ref, wr_ref, br_ref, wi_ref,
                       bi_ref, lam_ref, y_ref, h_out_ref, *, seg_len):
    rows, lanes = x_ref.shape
    x = x_ref[...]
    tmod = lax.broadcasted_iota(jnp.int32, (rows, lanes), 0) % seg_len
    prev = prev_ref[...]

    def prev_for_shift(k):
        return pltpu.roll(prev, rows + k - seg_len, 0)

    xc = _conv_taps(x, tmod, prev_for_shift, cw_ref, cb_ref)
    a, u = _lru_gates(xc, wr_ref, br_ref, wi_ref, bi_ref, lam_ref)
    a_cum, b_cum = _segmented_scan(a, u, tmod, seg_len)
    h = a_cum * h0_ref[...] + b_cum
    y_ref[...] = (h * _gelu_tanh(xg_ref[...])).astype(y_ref.dtype)
    h_out_ref[...] = h


def _lru_weight_specs(idx):
    nb = LRU_LANES // LRU_BLOCK_DIM
    vec = lambda rws: pl.BlockSpec((rws, LRU_LANES), lambda *g: (0, idx(*g)))
    mat = pl.BlockSpec((nb, LRU_BLOCK_DIM, LRU_BLOCK_DIM), lambda *g: (idx(*g), 0, 0))
    return [vec(CONV_WIDTH), vec(1), mat, vec(1), mat, vec(1), vec(1)]


def _lru_prompt(proj, xr_col0, xg_col0, n_batch, seq, weights):
    width = weights[1].shape[1]
    nt = seq // LRU_ROWS
    nl = width // LRU_LANES
    xr_blk, xg_blk = xr_col0 // LRU_LANES, xg_col0 // LRU_LANES
    y, h_last = pl.pallas_call(
        _lru_prompt_kernel,
        out_shape=(jax.ShapeDtypeStruct((n_batch * seq, width), BF16),
                   jax.ShapeDtypeStruct((n_batch, 1, width), F32)),
        grid=(n_batch, nl, nt),
        in_specs=[pl.BlockSpec((LRU_ROWS, LRU_LANES), lambda b, j, t: (b * nt + t, xr_blk + j)),
                  pl.BlockSpec((LRU_ROWS, LRU_LANES), lambda b, j, t: (b * nt + t, xg_blk + j))]
                 + _lru_weight_specs(lambda b, j, t: j),
        out_specs=(pl.BlockSpec((LRU_ROWS, LRU_LANES), lambda b, j, t: (b * nt + t, j)),
                   pl.BlockSpec((None, 1, LRU_LANES), lambda b, j, t: (b, 0, j))),
        scratch_shapes=[pltpu.VMEM((SUBLANES + LRU_ROWS, LRU_LANES), F32),
                        pltpu.VMEM((SUBLANES, LRU_LANES), F32)],
        compiler_params=_params("parallel", "parallel", "arbitrary"),
        name="lru_prompt",
    )(proj, proj, *weights)
    return y, h_last.reshape(n_batch, width)


def _lru_sample(proj, row0, xr_col0, xg_col0, n_rows, seg_len, prev_rows, h0_rows, weights):
    width = weights[1].shape[1]
    nr = n_rows // LRU_ROWS
    nl = width // LRU_LANES
    r_blk = row0 // LRU_ROWS
    xr_blk, xg_blk = xr_col0 // LRU_LANES, xg_col0 // LRU_LANES
    tile = lambda: pl.BlockSpec((LRU_ROWS, LRU_LANES), lambda i, j: (i, j))
    return pl.pallas_call(
        functools.partial(_lru_sample_kernel, seg_len=seg_len),
        out_shape=(jax.ShapeDtypeStruct((n_rows, width), BF16),
                   jax.ShapeDtypeStruct((n_rows, width), F32)),
        grid=(nr, nl),
        in_specs=[pl.BlockSpec((LRU_ROWS, LRU_LANES), lambda i, j: (r_blk + i, xr_blk + j)),
                  pl.BlockSpec((LRU_ROWS, LRU_LANES), lambda i, j: (r_blk + i, xg_blk + j)),
                  tile(), tile()] + _lru_weight_specs(lambda i, j: j),
        out_specs=(tile(), tile()),
        compiler_params=_params("parallel", "parallel"),
        name="lru_sample",
    )(proj, proj, prev_rows, h0_rows, *weights)


def _merge_kernel(*refs, o_tiles, y_tiles):
    o_refs = refs[:len(o_tiles)]
    y_refs = refs[len(o_tiles):len(o_tiles) + len(y_tiles)]
    wa_ref, wl_ref, ga_ref, gl_ref, ba_ref, bl_ref, out_ref = refs[len(o_tiles) + len(y_tiles):]
    i = pl.program_id(0)
    a_out = jnp.dot(_select_part(i, o_refs, o_tiles), wa_ref[...].astype(BF16),
                    preferred_element_type=F32)
    r_out = jnp.dot(_select_part(i, y_refs, y_tiles), wl_ref[...].astype(BF16),
                    preferred_element_type=F32)
    g_a = jax.nn.sigmoid(ga_ref[...] + ba_ref[...])
    g_l = jax.nn.sigmoid(gl_ref[...] + bl_ref[...])
    out_ref[...] = (g_a * a_out + g_l * r_out).astype(out_ref.dtype)


def _merge(o_parts, y_parts, w_attn_o, w_lru_o, proj, gate_col0, b_gate, *, tn=512):
    ka, kl = o_parts[0].shape[1], y_parts[0].shape[1]
    m = sum(p.shape[0] for p in o_parts)
    d = w_attn_o.shape[1]
    g0 = gate_col0 // tn
    nd = d // tn
    once = dict(pipeline_mode=pl.Buffered(1))
    return pl.pallas_call(
        functools.partial(_merge_kernel, o_tiles=_part_tiles(o_parts, ROW_TILE),
                          y_tiles=_part_tiles(y_parts, ROW_TILE)),
        out_shape=jax.ShapeDtypeStruct((m, d), BF16),
        grid=(m // ROW_TILE, nd),
        in_specs=_stacked_row_specs(o_parts, ROW_TILE, ka, lambda i, j: 0, **once)
                 + _stacked_row_specs(y_parts, ROW_TILE, kl, lambda i, j: 0, **once)
                 + [pl.BlockSpec((ka, tn), lambda i, j: (0, j)),
                    pl.BlockSpec((kl, tn), lambda i, j: (0, j)),
                    pl.BlockSpec((ROW_TILE, tn), lambda i, j: (i, g0 + j)),
                    pl.BlockSpec((ROW_TILE, tn), lambda i, j: (i, g0 + nd + j)),
                    pl.BlockSpec((1, tn), lambda i, j: (0, j)),
                    pl.BlockSpec((1, tn), lambda i, j: (0, nd + j))],
        out_specs=pl.BlockSpec((ROW_TILE, tn), lambda i, j: (i, j)),
        compiler_params=_params("arbitrary", "arbitrary"),
        name="merge",
    )(*o_parts, *y_parts, w_attn_o, w_lru_o, proj, proj, b_gate, b_gate)


def _out_proj_kernel(*refs, x_tiles):
    a_ref, w_ref, *x_refs, o_ref = refs
    x = _select_part(pl.program_id(0), x_refs, x_tiles)
    o_ref[...] = x + jnp.dot(a_ref[...], w_ref[...].astype(BF16), preferred_element_type=F32)


def _out_proj(a, w, x_parts, *, tn=512):
    m, k = a.shape
    n = w.shape[1]
    return pl.pallas_call(
        functools.partial(_out_proj_kernel, x_tiles=_part_tiles(x_parts, ROW_TILE)),
        out_shape=jax.ShapeDtypeStruct((m, n), F32),
        grid=(m // ROW_TILE, n // tn),
        in_specs=[pl.BlockSpec((ROW_TILE, k), lambda i, j: (i, 0)),
                  pl.BlockSpec((k, tn), lambda i, j: (0, j))]
                 + _stacked_row_specs(x_parts, ROW_TILE, tn, lambda i, j: j),
        out_specs=pl.BlockSpec((ROW_TILE, tn), lambda i, j: (i, j)),
        compiler_params=_params("arbitrary", "arbitrary"),
        name="out_proj",
    )(a, w, *x_parts)


def _swiglu_down_kernel(a0_ref, a1_ref, w_ref, x_ref, *o_refs, o_tiles):
    i, k = pl.program_id(0), pl.program_id(2)
    w = w_ref[...].astype(BF16)

    def emit(a_ref, first):
        part = jnp.dot(a_ref[...], w, preferred_element_type=F32)
        lo = 0
        for o_ref, n in zip(o_refs, o_tiles):
            @pl.when((i >= lo) & (i < lo + n))
            def _():
                if first:
                    o_ref[...] = x_ref[...] + part
                else:
                    o_ref[...] += part
            lo += n

    @pl.when(k == 0)
    def _():
        emit(a0_ref, True)

    @pl.when(k == 1)
    def _():
        emit(a1_ref, False)


def _swiglu_down(a, w, x, out_rows, *, tn=256):
    m, k = a.shape
    n = w.shape[1]
    tk = k // 2
    nj = n // tn
    o_tiles = tuple(r // ROW_TILE for r in out_rows)
    once = dict(pipeline_mode=pl.Buffered(1))

    def out_spec(lo, cnt):
        def index(i, j, s):
            mine = (i >= lo) & (i < lo + cnt)
            return (jnp.clip(i - lo, 0, cnt - 1), jnp.where(mine, j, jnp.where(i < lo, 0, nj - 1)))
        return pl.BlockSpec((ROW_TILE, tn), index)

    out_specs, lo = [], 0
    for cnt in o_tiles:
        out_specs.append(out_spec(lo, cnt))
        lo += cnt
    return pl.pallas_call(
        functools.partial(_swiglu_down_kernel, o_tiles=o_tiles),
        out_shape=tuple(jax.ShapeDtypeStruct((r, n), F32) for r in out_rows),
        grid=(m // ROW_TILE, nj, 2),
        in_specs=[pl.BlockSpec((ROW_TILE, tk), lambda i, j, s: (i, 0), **once),
                  pl.BlockSpec((ROW_TILE, tk), lambda i, j, s: (i, 1), **once),
                  pl.BlockSpec((tk, tn), lambda i, j, s: (s, j)),
                  pl.BlockSpec((ROW_TILE, tn), lambda i, j, s: (i, j))],
        out_specs=tuple(out_specs),
        compiler_params=_params("arbitrary", "arbitrary", "arbitrary"),
        name="swiglu_down",
    )(a, a, w, x)


def _swiglu_up_kernel(x_ref, wg_ref, wu_ref, o_ref):
    x = x_ref[...]
    g = jnp.dot(x, wg_ref[...].astype(BF16), preferred_element_type=F32)
    u = jnp.dot(x, wu_ref[...].astype(BF16), preferred_element_type=F32)
    o_ref[...] = (jax.nn.silu(g) * u).astype(o_ref.dtype)


def _swiglu_up(xn, wg, wu, *, tn=256):
    m, d = xn.shape
    f = wg.shape[1]
    return pl.pallas_call(
        _swiglu_up_kernel,
        out_shape=jax.ShapeDtypeStruct((m, f), BF16),
        grid=(m // ROW_TILE, f // tn),
        in_specs=[pl.BlockSpec((ROW_TILE, d), lambda i, j: (i, 0)),
                  pl.BlockSpec((d, tn), lambda i, j: (0, j)),
                  pl.BlockSpec((d, tn), lambda i, j: (0, j))],
        out_specs=pl.BlockSpec((ROW_TILE, tn), lambda i, j: (i, j)),
        compiler_params=_params("parallel", "arbitrary"),
        name="swiglu_up",
    )(xn, wg, wu)


def kernel(x_prompt, x_sample, cache_k, cache_v, state_h, state_conv, page_table, norm_mix, w_in,
           b_gate, q_norm, k_norm, lambda_q1, lambda_k1, lambda_q2, lambda_k2, rel_bias,
           attn_subln, w_attn_o, conv_w, conv_b, w_r, b_r, w_i, b_i, lru_lambda, w_lru_o, w_out,
           norm_ffn, w_ffn_gate, w_ffn_up, w_ffn_down):
    bp, tp, d = x_prompt.shape
    bs, ts, _ = x_sample.shape
    depth = w_in.shape[0]
    assert depth == 1
    l = 0
    mp, ms = bp * tp, bs * ts
    qk_w = N_HEADS * HEAD_W
    lru_w = conv_b.shape[1]
    n_pages, page = page_table.shape[1], cache_k.shape[2]
    past = n_pages * page
    col_v, col_xr, col_xg, col_gate = 2 * qk_w, 3 * qk_w, 3 * qk_w + lru_w, 3 * qk_w + 2 * lru_w

    x_parts = (x_prompt.reshape(mp, d), x_sample.reshape(ms, d))

    xn = _rmsnorm(x_parts, norm_mix[l])
    qk_gain = jnp.concatenate([jnp.tile(q_norm[l], 2 * N_HEADS), jnp.tile(k_norm[l], 2 * N_HEADS)])
    proj = _in_proj(xn, w_in[l], qk_gain.reshape(1, 2 * qk_w))

    pattern_pages, page_pattern = _page_patterns(past, page, n_pages, ts)
    bias_tiles, bias_far, bias_pages, bias_new = _bias_tables(rel_bias, past, page, ts, pattern_pages)
    lams = [v[l].reshape(1, D_HEAD) for v in (lambda_q1, lambda_k1, lambda_q2, lambda_k2)]
    subln = attn_subln[l]
    o_prompt = _prompt_attention(proj, bp, tp, bias_tiles, bias_far,
                                 subln.reshape(N_HEADS, 1, HEAD_W), lams)
    as_head_rows = lambda a: a.reshape(ms * N_HEADS, HEAD_W)
    o_sample = _sample_attention(as_head_rows(proj[mp:, :qk_w]), as_head_rows(proj[mp:, qk_w:2 * qk_w]),
                                 as_head_rows(proj[mp:, col_v:col_v + qk_w]), bs,
                                 cache_k, cache_v, l, page_table, jnp.asarray(page_pattern),
                                 bias_pages, bias_new, jnp.tile(subln, (ts, 1)), lams).reshape(ms, qk_w)

    lru_weights = (conv_w[l], conv_b[l].reshape(1, lru_w), w_r[l], b_r[l].reshape(1, lru_w),
                   w_i[l], b_i[l].reshape(1, lru_w), lru_lambda[l].reshape(1, lru_w))
    y_prompt, h_prompt = _lru_prompt(proj, col_xr, col_xg, bp, tp, lru_weights)
    prev_rows = jnp.pad(state_conv[l], ((0, 0), (ts - (CONV_WIDTH - 1), 0), (0, 0))).reshape(ms, lru_w)
    h0_rows = jnp.repeat(state_h[l], ts, axis=0)
    y_sample, h_rows = _lru_sample(proj, mp, col_xr, col_xg, ms, ts, prev_rows, h0_rows, lru_weights)

    merged = _merge((o_prompt, o_sample), (y_prompt, y_sample), w_attn_o[l], w_lru_o[l],
                    proj, col_gate, b_gate[l].reshape(1, 2 * d))
    x1 = _out_proj(merged, w_out[l], x_parts)
    xn2 = _rmsnorm((x1,), norm_ffn[l])
    hid = _swiglu_up(xn2, w_ffn_gate[l], w_ffn_up[l])
    out_prompt, out_sample = _swiglu_down(hid, w_ffn_down[l], x1, (mp, ms))

    k_all = proj[:, qk_w:2 * qk_w]
    v_all = proj[:, col_v:col_v + qk_w]
    xr_all = proj[:, col_xr:col_xr + lru_w]
    tail = CONV_WIDTH - 1
    return (out_prompt.reshape(bp, tp, d), out_sample.reshape(bs, ts, d),
            k_all[:mp].reshape(1, bp, tp, N_HEADS, HEAD_W), v_all[:mp].reshape(1, bp, tp, N_HEADS, HEAD_W),
            h_prompt[None].astype(state_h.dtype),
            xr_all[:mp].reshape(bp, tp, lru_w)[None, :, tp - tail:, :].astype(state_conv.dtype),
            k_all[mp:].reshape(1, bs, ts, N_HEADS, HEAD_W), v_all[mp:].reshape(1, bs, ts, N_HEADS, HEAD_W),
            h_rows.reshape(bs, ts, lru_w)[None, :, ts - 1, :].astype(state_h.dtype),
            xr_all[mp:].reshape(bs, ts, lru_w)[None, :, ts - tail:, :].astype(state_conv.dtype))
```

```python
import functools
import math

import jax
import jax.numpy as jnp
import numpy as np
from jax import lax
from jax.experimental import pallas as pl
from jax.experimental.pallas import tpu as pltpu

F32 = jnp.float32
BF16 = jnp.bfloat16

N_HEADS = 8
D_HEAD = 128
HEAD_W = 2 * D_HEAD
LRU_BLOCK_DIM = 128
CONV_WIDTH = 4
LRU_C = 8.0
NUM_BUCKETS = 32
MAX_DISTANCE = 128
EPS = 1e-6
NEG_INF = -1e30
LAYER = 0
LAM_INIT = 0.8 - 0.6 * math.exp(-0.3 * LAYER)

VMEM_LIMIT_BYTES = 56 * 1024 * 1024
LANES = 128
SUBLANES = 8

ROW_TILE = 1024
NORM_ROWS = 256
ATTN_TQ = 256
ATTN_TK = 256
PAGES_PER_STEP = 8
LRU_ROWS = 512
LRU_LANES = 512


def _params(*semantics):
    return pltpu.CompilerParams(dimension_semantics=semantics,
                                vmem_limit_bytes=VMEM_LIMIT_BYTES)


def _lane_tile(x, width):
    return jnp.concatenate([x] * (width // LANES), axis=1)


def _stacked_row_specs(parts, tm, cols, col_index, **spec_kwargs):
    specs, lo = [], 0
    for part in parts:
        n = part.shape[0] // tm
        specs.append(pl.BlockSpec(
            (tm, cols),
            lambda i, *g, lo=lo, n=n: (jnp.clip(i - lo, 0, n - 1), col_index(i, *g)),
            **spec_kwargs))
        lo += n
    return specs


def _part_tiles(parts, tm):
    return tuple(p.shape[0] // tm for p in parts)


def _select_part(i, refs, part_tiles):
    x = refs[-1][...]
    hi = sum(part_tiles[:-1])
    for ref, n in zip(reversed(refs[:-1]), reversed(part_tiles[:-1])):
        x = jnp.where(i < hi, ref[...], x)
        hi -= n
    return x


def _rmsnorm_kernel(*refs, part_tiles):
    *x_refs, g_ref, o_ref = refs
    x = _select_part(pl.program_id(0), x_refs, part_tiles)
    ms = jnp.mean(x * x, axis=-1, keepdims=True)
    o_ref[...] = (x * lax.rsqrt(ms + EPS) * g_ref[...]).astype(o_ref.dtype)


def _rmsnorm(parts, g):
    d = parts[0].shape[1]
    m = sum(p.shape[0] for p in parts)
    return pl.pallas_call(
        functools.partial(_rmsnorm_kernel, part_tiles=_part_tiles(parts, NORM_ROWS)),
        out_shape=jax.ShapeDtypeStruct((m, d), BF16),
        grid=(m // NORM_ROWS,),
        in_specs=_stacked_row_specs(parts, NORM_ROWS, d, lambda i: 0)
                 + [pl.BlockSpec((1, d), lambda i: (0, 0))],
        out_specs=pl.BlockSpec((NORM_ROWS, d), lambda i: (i, 0)),
        compiler_params=_params("parallel"),
        name="rmsnorm",
    )(*parts, g.reshape(1, d))


def _in_proj_kernel(x_ref, w_ref, g_ref, o_ref, *, n_norm_tiles):
    j = pl.program_id(1)
    acc = jnp.dot(x_ref[...], w_ref[...].astype(BF16), preferred_element_type=F32)

    @pl.when(j < n_norm_tiles)
    def _():
        for c in range(0, acc.shape[1], D_HEAD):
            seg = acc[:, c:c + D_HEAD]
            ms = jnp.mean(seg * seg, axis=-1, keepdims=True)
            o_ref[:, c:c + D_HEAD] = seg * lax.rsqrt(ms + EPS) * g_ref[:, c:c + D_HEAD]

    @pl.when(j >= n_norm_tiles)
    def _():
        o_ref[...] = acc


def _in_proj(xn, w, qk_gain, *, tn=512):
    m, d = xn.shape
    n = w.shape[1]
    n_norm_tiles = qk_gain.shape[1] // tn
    return pl.pallas_call(
        functools.partial(_in_proj_kernel, n_norm_tiles=n_norm_tiles),
        out_shape=jax.ShapeDtypeStruct((m, n), F32),
        grid=(m // ROW_TILE, n // tn),
        in_specs=[pl.BlockSpec((ROW_TILE, d), lambda i, j: (i, 0)),
                  pl.BlockSpec((d, tn), lambda i, j: (0, j)),
                  pl.BlockSpec((1, tn), lambda i, j: (0, jnp.minimum(j, n_norm_tiles - 1)))],
        out_specs=pl.BlockSpec((ROW_TILE, tn), lambda i, j: (i, j)),
        compiler_params=_params("parallel", "arbitrary"),
        name="in_proj",
    )(xn, w, qk_gain)


def _bucket(dist):
    n = jnp.maximum(dist, 0)
    max_exact = NUM_BUCKETS // 2
    nf = jnp.maximum(n, max_exact).astype(F32)
    large = max_exact + (jnp.log(nf / max_exact) / math.log(MAX_DISTANCE / max_exact)
                         * (NUM_BUCKETS - max_exact)).astype(jnp.int32)
    return jnp.where(n < max_exact, n, jnp.minimum(large, NUM_BUCKETS - 1))


def _bucket_np(dist):
    n = np.maximum(dist, 0)
    max_exact = NUM_BUCKETS // 2
    nf = np.maximum(n, max_exact).astype(np.float32)
    large = max_exact + (np.log(nf / max_exact) / np.float32(math.log(MAX_DISTANCE / max_exact))
                         * (NUM_BUCKETS - max_exact)).astype(np.int32)
    return np.where(n < max_exact, n, np.minimum(large, NUM_BUCKETS - 1))


def _bias_tables_kernel(rb_ref, rbv_ref, ptile_ref, pfar_ref, spage_ref, snew_ref,
                        *, past, page, n_tok, pattern_pages):
    tq, tk = ptile_ref.shape[2], ptile_ref.shape[3]
    r = lax.broadcasted_iota(jnp.int32, (tq, tk), 0)
    c = lax.broadcasted_iota(jnp.int32, (tq, tk), 1)
    tile_buckets = (_bucket(r - c), _bucket(tk + r - c))
    far_bucket = _bucket(jnp.full((1, tk), 2 * tk, jnp.int32))

    def lookup_scalar(bucket, h):
        out = jnp.zeros(bucket.shape, F32)
        for b in range(NUM_BUCKETS):
            out = jnp.where(bucket == b, rb_ref[b, h], out)
        return out

    def per_head(h, carry):
        for t in range(2):
            ptile_ref[h, t] = lookup_scalar(tile_buckets[t], h)
        pfar_ref[h] = lookup_scalar(far_bucket, h)
        return carry
    lax.fori_loop(0, N_HEADS, per_head, 0)

    rows = snew_ref.shape[0]

    def lookup_rows(bucket):
        out = jnp.zeros(bucket.shape, F32)
        for b in range(NUM_BUCKETS):
            vals = _lane_tile(rbv_ref[b], bucket.shape[1])
            vals = jnp.concatenate([vals] * (bucket.shape[0] // SUBLANES), axis=0)
            out = jnp.where(bucket == b, vals, out)
        return out

    def sample_tile(width, dist_of):
        row = lax.broadcasted_iota(jnp.int32, (rows, width), 0)
        lane = lax.broadcasted_iota(jnp.int32, (rows, width), 1)
        tok = (row // N_HEADS) % n_tok
        key = lane // N_HEADS
        same_head = (row % N_HEADS) == (lane % N_HEADS)
        return tok, key, same_head, lookup_rows(_bucket(dist_of(tok, key)))

    for u, pg in enumerate(pattern_pages):
        tok, key, same_head, b = sample_tile(page * N_HEADS, lambda t, k: past + t - (pg * page + k))
        spage_ref[u] = jnp.where(same_head, b, NEG_INF)
    tok, key, same_head, b = sample_tile(snew_ref.shape[1], lambda t, k: t - k)
    snew_ref[...] = jnp.where(same_head & (key <= tok), b, NEG_INF)


def _bias_tables(rel_bias, past, page, n_tok, pattern_pages):
    rows = 2 * n_tok * N_HEADS
    rbv = jnp.broadcast_to(rel_bias[:, :, None], (NUM_BUCKETS, N_HEADS, LANES))
    return pl.pallas_call(
        functools.partial(_bias_tables_kernel, past=past, page=page, n_tok=n_tok,
                          pattern_pages=pattern_pages),
        out_shape=(jax.ShapeDtypeStruct((N_HEADS, 2, ATTN_TQ, ATTN_TK), F32),
                   jax.ShapeDtypeStruct((N_HEADS, 1, ATTN_TK), F32),
                   jax.ShapeDtypeStruct((len(pattern_pages), rows, page * N_HEADS), F32),
                   jax.ShapeDtypeStruct((rows, 2 * n_tok * N_HEADS), F32)),
        in_specs=[pl.BlockSpec(memory_space=pltpu.SMEM),
                  pl.BlockSpec(memory_space=pltpu.VMEM)],
        compiler_params=pltpu.CompilerParams(vmem_limit_bytes=VMEM_LIMIT_BYTES),
        name="bias_tables",
    )(rel_bias, rbv)


def _page_patterns(past, page, n_pages, n_tok):
    tok = np.arange(n_tok)[:, None]
    pos = np.arange(page)[None, :]
    keys = [_bucket_np(past + tok - (pg * page + pos)).tobytes() for pg in range(n_pages)]
    first = {}
    for pg, k in enumerate(keys):
        first.setdefault(k, pg)
    pattern_pages = tuple(sorted(first.values()))
    ids = np.array([pattern_pages.index(first[k]) for k in keys], np.int32)
    return pattern_pages, ids


def _diff_lambda(lq1, lk1, lq2, lk2):
    s1 = jnp.sum(lq1[...] * lk1[...], axis=-1, keepdims=True)
    s2 = jnp.sum(lq2[...] * lk2[...], axis=-1, keepdims=True)
    return jnp.exp(s1) - jnp.exp(s2) + LAM_INIT


def _head_subnorm(o, g):
    ms = jnp.mean(o * o, axis=-1, keepdims=True)
    return o * lax.rsqrt(ms + EPS) * g * (1.0 - LAM_INIT)


def _prompt_attn_kernel(q_ref, k_ref, v_ref, bias_ref, bfar_ref, g_ref, lq1, lk1, lq2, lk2,
                        o_ref, kb, vb):
    tq, tk = ATTN_TQ, ATTN_TK
    seq = q_ref.shape[0]
    scale = D_HEAD ** -0.5
    kb[...] = k_ref[...].astype(BF16)
    vb[...] = v_ref[...].astype(BF16)
    lam = _diff_lambda(lq1, lk1, lq2, lk2)
    row = lax.broadcasted_iota(jnp.int32, (tq, tk), 0)
    col = lax.broadcasted_iota(jnp.int32, (tq, tk), 1)
    causal = col <= row

    for i in range(seq // tq):
        n_keys = (i + 1) * tq
        n_chunks = n_keys // tk
        q = q_ref[i * tq:(i + 1) * tq, :]
        halves = []
        for off in (0, D_HEAD):
            s = lax.dot_general(q[:, off:off + D_HEAD].astype(BF16), kb[0:n_keys, off:off + D_HEAD],
                                (((1,), (1,)), ((), ())), preferred_element_type=F32) * scale
            chunks = []
            for c in range(n_chunks):
                sc = s[:, c * tk:(c + 1) * tk]
                if c == n_chunks - 1:
                    sc = jnp.where(causal, sc + bias_ref[0], NEG_INF)
                elif c == n_chunks - 2:
                    sc = sc + bias_ref[1]
                else:
                    sc = sc + bfar_ref[...]
                chunks.append(sc)
            m = chunks[0]
            for sc in chunks[1:]:
                m = jnp.maximum(m, sc)
            m = jnp.max(m, axis=-1, keepdims=True)
            p = [jnp.exp(sc - m) for sc in chunks]
            l = p[0]
            for pc in p[1:]:
                l = l + pc
            l = jnp.sum(l, axis=-1, keepdims=True)
            pv = jnp.dot(jnp.concatenate(p, axis=1).astype(BF16), vb[0:n_keys, :],
                         preferred_element_type=F32)
            halves.append(pv * (1.0 / l))
        o = halves[0] - lam * halves[1]
        o_ref[i * tq:(i + 1) * tq, :] = _head_subnorm(o, g_ref[...]).astype(o_ref.dtype)


def _prompt_attention(proj, n_batch, seq, bias_tiles, bias_far, subln, lams):
    k_col0 = N_HEADS
    v_col0 = 2 * N_HEADS
    vec = pl.BlockSpec((1, D_HEAD), lambda b, h: (0, 0))
    return pl.pallas_call(
        _prompt_attn_kernel,
        out_shape=jax.ShapeDtypeStruct((n_batch * seq, N_HEADS * HEAD_W), BF16),
        grid=(n_batch, N_HEADS),
        in_specs=[pl.BlockSpec((seq, HEAD_W), lambda b, h: (b, h)),
                  pl.BlockSpec((seq, HEAD_W), lambda b, h: (b, k_col0 + h)),
                  pl.BlockSpec((seq, HEAD_W), lambda b, h: (b, v_col0 + h)),
                  pl.BlockSpec((None, 2, ATTN_TQ, ATTN_TK), lambda b, h: (h, 0, 0, 0)),
                  pl.BlockSpec((None, 1, ATTN_TK), lambda b, h: (h, 0, 0)),
                  pl.BlockSpec((None, 1, HEAD_W), lambda b, h: (h, 0, 0)),
                  vec, vec, vec, vec],
        out_specs=pl.BlockSpec((seq, HEAD_W), lambda b, h: (b, h)),
        scratch_shapes=[pltpu.VMEM((seq, HEAD_W), BF16), pltpu.VMEM((seq, HEAD_W), BF16)],
        compiler_params=_params("parallel", "parallel"),
        name="prompt_attention",
    )(proj, proj, proj, bias_tiles, bias_far, subln, *lams)


def _sample_attn_kernel(pt_ref, pat_ref, q_ref, kn_ref, vn_ref, *rest):
    npg = PAGES_PER_STEP
    k_refs = rest[:npg]
    v_refs = rest[npg:2 * npg]
    (bias_ref, bias_new_ref, g_ref, lq1, lk1, lq2, lk2,
     o_ref, qmt, m_ref, l_ref, acc_ref) = rest[2 * npg:]
    del pt_ref
    p = pl.program_id(1)
    n_q = q_ref.shape[0]
    scale = D_HEAD ** -0.5

    def scores(k2d, bias):
        return lax.dot_general(qmt[...], k2d, (((1,), (1,)), ((), ())),
                               preferred_element_type=F32) * scale + bias

    def update(s_blocks, v_blocks):
        parts = []
        for s, v2d in zip(s_blocks, v_blocks):
            m_blk = jnp.max(s, axis=-1, keepdims=True)
            e = jnp.exp(s - m_blk)
            parts.append((m_blk, jnp.sum(e, axis=-1, keepdims=True),
                          jnp.dot(e.astype(BF16), v2d, preferred_element_type=F32)))
        m_old = m_ref[...]
        m_new = m_old
        for m_blk, _, _ in parts:
            m_new = jnp.maximum(m_new, m_blk)
        alpha = jnp.exp(m_old - m_new)
        l_new = alpha * l_ref[...]
        acc = _lane_tile(alpha, HEAD_W) * acc_ref[...]
        for m_blk, l_blk, pv in parts:
            w_blk = jnp.exp(m_blk - m_new)
            l_new = l_new + w_blk * l_blk
            acc = acc + _lane_tile(w_blk, HEAD_W) * pv
        m_ref[...] = m_new
        l_ref[...] = l_new
        acc_ref[...] = acc

    @pl.when(p == 0)
    def _():
        q = q_ref[...]
        lane = lax.broadcasted_iota(jnp.int32, q.shape, 1)
        qmt[...] = jnp.concatenate([jnp.where(lane < D_HEAD, q, 0.0),
                                    jnp.where(lane >= D_HEAD, q, 0.0)], axis=0).astype(BF16)
        m_ref[...] = jnp.full(m_ref.shape, NEG_INF, F32)
        l_ref[...] = jnp.zeros(l_ref.shape, F32)
        acc_ref[...] = jnp.zeros(acc_ref.shape, F32)
        pad = jnp.zeros((bias_new_ref.shape[1] - n_q, HEAD_W), F32)
        kn = jnp.concatenate([kn_ref[...], pad], axis=0).astype(BF16)
        vn = jnp.concatenate([vn_ref[...], pad], axis=0).astype(BF16)
        update([scores(kn, bias_new_ref[...])], [vn])

    s_blocks, v_blocks = [], []
    for i in range(npg):
        pos, heads, w = k_refs[i].shape
        k2d = k_refs[i][...].reshape(pos * heads, w).astype(BF16)
        s_blocks.append(scores(k2d, bias_ref[pat_ref[p * npg + i]]))
        v_blocks.append(v_refs[i][...].reshape(pos * heads, w).astype(BF16))
    update(s_blocks, v_blocks)

    @pl.when(p == pl.num_programs(1) - 1)
    def _():
        lam = _diff_lambda(lq1, lk1, lq2, lk2)
        inv_l = _lane_tile(1.0 / l_ref[...], HEAD_W)
        o_all = acc_ref[...] * inv_l
        o = o_all[0:n_q, :] - lam * o_all[n_q:2 * n_q, :]
        o_ref[...] = _head_subnorm(o, g_ref[...]).astype(o_ref.dtype)


def _sample_attention(q2d, kn2d, vn2d, n_seq, cache_k, cache_v, layer, page_table, page_pattern,
                      bias_pages, bias_new, subln_rows, lams):
    page, heads, w = cache_k.shape[2:]
    n_q = q2d.shape[0] // n_seq
    n_pages = page_table.shape[1]
    npg = PAGES_PER_STEP
    rows = 2 * n_q

    def page_spec(i):
        return pl.BlockSpec((None, None, page, heads, w),
                            lambda s, p, pt, pat: (layer, pt[s, p * npg + i], 0, 0, 0))

    seq_rows = pl.BlockSpec((n_q, w), lambda s, p, pt, pat: (s, 0))
    vec = pl.BlockSpec((1, D_HEAD), lambda s, p, pt, pat: (0, 0))
    in_specs = ([seq_rows, seq_rows, seq_rows]
                + [page_spec(i) for i in range(npg)]
                + [page_spec(i) for i in range(npg)]
                + [pl.BlockSpec(bias_pages.shape, lambda s, p, pt, pat: (0, 0, 0)),
                   pl.BlockSpec(bias_new.shape, lambda s, p, pt, pat: (0, 0)),
                   pl.BlockSpec((n_q, w), lambda s, p, pt, pat: (0, 0)),
                   vec, vec, vec, vec])
    return pl.pallas_call(
        _sample_attn_kernel,
        out_shape=jax.ShapeDtypeStruct((n_seq * n_q, w), BF16),
        grid_spec=pltpu.PrefetchScalarGridSpec(
            num_scalar_prefetch=2,
            grid=(n_seq, n_pages // npg),
            in_specs=in_specs,
            out_specs=seq_rows,
            scratch_shapes=[pltpu.VMEM((rows, w), BF16),
                            pltpu.VMEM((rows, LANES), F32), pltpu.VMEM((rows, LANES), F32),
                            pltpu.VMEM((rows, w), F32)]),
        compiler_params=_params("parallel", "arbitrary"),
        name="sample_attention",
    )(page_table, page_pattern, q2d, kn2d, vn2d, *([cache_k] * npg), *([cache_v] * npg),
      bias_pages, bias_new, subln_rows, *lams)


def _gelu_tanh(x):
    return 0.5 * x * (1.0 + jnp.tanh(math.sqrt(2.0 / math.pi) * (x + 0.044715 * (x * x * x))))


def _softplus(x):
    return jnp.maximum(x, 0.0) + jnp.log1p(jnp.exp(-jnp.abs(x)))


def _one_minus_exp2x(x):
    t = jnp.tanh(x)
    return -2.0 * t * (1.0 / (1.0 - t))


def _lru_gates(xc, wr_ref, br_ref, wi_ref, bi_ref, lam_ref):
    r_parts, i_parts = [], []
    for n in range(xc.shape[1] // LRU_BLOCK_DIM):
        xb = xc[:, n * LRU_BLOCK_DIM:(n + 1) * LRU_BLOCK_DIM].astype(BF16)
        r_parts.append(jnp.dot(xb, wr_ref[n].astype(BF16), preferred_element_type=F32))
        i_parts.append(jnp.dot(xb, wi_ref[n].astype(BF16), preferred_element_type=F32))
    r = jax.nn.sigmoid(jnp.concatenate(r_parts, axis=1) + br_ref[...])
    i = jax.nn.sigmoid(jnp.concatenate(i_parts, axis=1) + bi_ref[...])
    log_a = -LRU_C * r * _softplus(-lam_ref[...])
    a = jnp.exp(log_a)
    u = jnp.sqrt(_one_minus_exp2x(log_a)) * i * xc
    return a, u


def _segmented_scan(a, u, tmod, seg_len):
    s = 1
    while s < seg_len:
        keep = tmod >= s
        a_prev = jnp.where(keep, pltpu.roll(a, s, 0), 1.0)
        u_prev = jnp.where(keep, pltpu.roll(u, s, 0), 0.0)
        u = a * u_prev + u
        a = a * a_prev
        s *= 2
    return a, u


def _chained_scan(a, u, tmod, h_in):
    a_loc, b_loc = _segmented_scan(a, u, tmod % SUBLANES, SUBLANES)
    groups, carry = [], h_in
    for g in range(a.shape[0] // SUBLANES):
        rows = slice(g * SUBLANES, (g + 1) * SUBLANES)
        h_g = a_loc[rows] * carry + b_loc[rows]
        carry = h_g[SUBLANES - 1:SUBLANES]
        groups.append(h_g)
    return jnp.concatenate(groups, axis=0)


def _conv_taps(x, tmod, prev_for_shift, cw_ref, cb_ref):
    xc = cb_ref[...] + cw_ref[CONV_WIDTH - 1:CONV_WIDTH, :] * x
    for k in range(1, CONV_WIDTH):
        shifted = jnp.where(tmod >= k, pltpu.roll(x, k, 0), prev_for_shift(k))
        xc = xc + cw_ref[CONV_WIDTH - 1 - k:CONV_WIDTH - k, :] * shifted
    return xc


def _lru_prompt_kernel(x_ref, xg_ref, cw_ref, cb_ref, wr_ref, br_ref, wi_ref, bi_ref, lam_ref,
                       y_ref, hlast_ref, xpad_ref, h_ref):
    t = pl.program_id(2)
    rows, lanes = x_ref.shape

    @pl.when(t == 0)
    def _():
        xpad_ref[0:SUBLANES, :] = jnp.zeros((SUBLANES, lanes), F32)
        h_ref[...] = jnp.zeros(h_ref.shape, F32)

    x = x_ref[...]
    xpad_ref[SUBLANES:SUBLANES + rows, :] = x
    xc = cb_ref[...] + cw_ref[CONV_WIDTH - 1:CONV_WIDTH, :] * x
    for k in range(1, CONV_WIDTH):
        xc = xc + (cw_ref[CONV_WIDTH - 1 - k:CONV_WIDTH - k, :]
                   * xpad_ref[SUBLANES - k:SUBLANES - k + rows, :])
    tmod = lax.broadcasted_iota(jnp.int32, (rows, lanes), 0)
    a, u = _lru_gates(xc, wr_ref, br_ref, wi_ref, bi_ref, lam_ref)
    h = _chained_scan(a, u, tmod, h_ref[0:1, :])
    y_ref[...] = (h * _gelu_tanh(xg_ref[...])).astype(y_ref.dtype)
    h_last = h[rows - 1:rows, :]
    h_ref[...] = jnp.broadcast_to(h_last, h_ref.shape)
    hlast_ref[...] = h_last
    xpad_ref[0:SUBLANES, :] = x[rows - SUBLANES:rows, :]


def _lru_sample_kernel(x_ref, xg_ref, prev_ref, h0_ref, cw_ref, cb_ref, wr_ref, br_ref, wi_ref,
                       bi_ref, lam_ref, y_ref, h_out_ref, *, seg_len):
    rows, lanes = x_ref.shape
    x = x_ref[...]
    tmod = lax.broadcasted_iota(jnp.int32, (rows, lanes), 0) % seg_len
    prev = prev_ref[...]

    def prev_for_shift(k):
        return pltpu.roll(prev, rows + k - seg_len, 0)

    xc = _conv_taps(x, tmod, prev_for_shift, cw_ref, cb_ref)
    a, u = _lru_gates(xc, wr_ref, br_ref, wi_ref, bi_ref, lam_ref)
    a_cum, b_cum = _segmented_scan(a, u, tmod, seg_len)
    h = a_cum * h0_ref[...] + b_cum
    y_ref[...] = (h * _gelu_tanh(xg_ref[...])).astype(y_ref.dtype)
    h_out_ref[...] = h


def _lru_weight_specs(idx):
    nb = LRU_LANES // LRU_BLOCK_DIM
    vec = lambda rws: pl.BlockSpec((rws, LRU_LANES), lambda *g: (0, idx(*g)))
    mat = pl.BlockSpec((nb, LRU_BLOCK_DIM, LRU_BLOCK_DIM), lambda *g: (idx(*g), 0, 0))
    return [vec(CONV_WIDTH), vec(1), mat, vec(1), mat, vec(1), vec(1)]


def _lru_prompt(proj, xr_col0, xg_col0, n_batch, seq, weights):
    width = weights[1].shape[1]
    nt = seq // LRU_ROWS
    nl = width // LRU_LANES
    xr_blk, xg_blk = xr_col0 // LRU_LANES, xg_col0 // LRU_LANES
    y, h_last = pl.pallas_call(
        _lru_prompt_kernel,
        out_shape=(jax.ShapeDtypeStruct((n_batch * seq, width), BF16),
                   jax.ShapeDtypeStruct((n_batch, 1, width), F32)),
        grid=(n_batch, nl, nt),
        in_specs=[pl.BlockSpec((LRU_ROWS, LRU_LANES), lambda b, j, t: (b * nt + t, xr_blk + j)),
                  pl.BlockSpec((LRU_ROWS, LRU_LANES), lambda b, j, t: (b * nt + t, xg_blk + j))]
                 + _lru_weight_specs(lambda b, j, t: j),
        out_specs=(pl.BlockSpec((LRU_ROWS, LRU_LANES), lambda b, j, t: (b * nt + t, j)),
                   pl.BlockSpec((None, 1, LRU_LANES), lambda b, j, t: (b, 0, j))),
        scratch_shapes=[pltpu.VMEM((SUBLANES + LRU_ROWS, LRU_LANES), F32),
                        pltpu.VMEM((SUBLANES, LRU_LANES), F32)],
        compiler_params=_params("parallel", "parallel", "arbitrary"),
        name="lru_prompt",
    )(proj, proj, *weights)
    return y, h_last.reshape(n_batch, width)


def _lru_sample(proj, row0, xr_col0, xg_col0, n_rows, seg_len, prev_rows, h0_rows, weights):
    width = weights[1].shape[1]
    nr = n_rows // LRU_ROWS
    nl = width // LRU_LANES
    r_blk = row0 // LRU_ROWS
    xr_blk, xg_blk = xr_col0 // LRU_LANES, xg_col0 // LRU_LANES
    tile = lambda: pl.BlockSpec((LRU_ROWS, LRU_LANES), lambda i, j: (i, j))
    return pl.pallas_call(
        functools.partial(_lru_sample_kernel, seg_len=seg_len),
        out_shape=(jax.ShapeDtypeStruct((n_rows, width), BF16),
                   jax.ShapeDtypeStruct((n_rows, width), F32)),
        grid=(nr, nl),
        in_specs=[pl.BlockSpec((LRU_ROWS, LRU_LANES), lambda i, j: (r_blk + i, xr_blk + j)),
                  pl.BlockSpec((LRU_ROWS, LRU_LANES), lambda i, j: (r_blk + i, xg_blk + j)),
                  tile(), tile()] + _lru_weight_specs(lambda i, j: j),
        out_specs=(tile(), tile()),
        compiler_params=_params("parallel", "parallel"),
        name="lru_sample",
    )(proj, proj, prev_rows, h0_rows, *weights)


def _merge_kernel(*refs, o_tiles, y_tiles):
    o_refs = refs[:len(o_tiles)]
    y_refs = refs[len(o_tiles):len(o_tiles) + len(y_tiles)]
    wa_ref, wl_ref, ga_ref, gl_ref, ba_ref, bl_ref, out_ref = refs[len(o_tiles) + len(y_tiles):]
    i = pl.program_id(0)
    a_out = jnp.dot(_select_part(i, o_refs, o_tiles), wa_ref[...].astype(BF16),
                    preferred_element_type=F32)
    r_out = jnp.dot(_select_part(i, y_refs, y_tiles), wl_ref[...].astype(BF16),
                    preferred_element_type=F32)
    g_a = jax.nn.sigmoid(ga_ref[...] + ba_ref[...])
    g_l = jax.nn.sigmoid(gl_ref[...] + bl_ref[...])
    out_ref[...] = (g_a * a_out + g_l * r_out).astype(out_ref.dtype)


def _merge(o_parts, y_parts, w_attn_o, w_lru_o, proj, gate_col0, b_gate, *, tn=512):
    ka, kl = o_parts[0].shape[1], y_parts[0].shape[1]
    m = sum(p.shape[0] for p in o_parts)
    d = w_attn_o.shape[1]
    g0 = gate_col0 // tn
    nd = d // tn
    once = dict(pipeline_mode=pl.Buffered(1))
    return pl.pallas_call(
        functools.partial(_merge_kernel, o_tiles=_part_tiles(o_parts, ROW_TILE),
                          y_tiles=_part_tiles(y_parts, ROW_TILE)),
        out_shape=jax.ShapeDtypeStruct((m, d), BF16),
        grid=(m // ROW_TILE, nd),
        in_specs=_stacked_row_specs(o_parts, ROW_TILE, ka, lambda i, j: 0, **once)
                 + _stacked_row_specs(y_parts, ROW_TILE, kl, lambda i, j: 0, **once)
                 + [pl.BlockSpec((ka, tn), lambda i, j: (0, j)),
                    pl.BlockSpec((kl, tn), lambda i, j: (0, j)),
                    pl.BlockSpec((ROW_TILE, tn), lambda i, j: (i, g0 + j)),
                    pl.BlockSpec((ROW_TILE, tn), lambda i, j: (i, g0 + nd + j)),
                    pl.BlockSpec((1, tn), lambda i, j: (0, j)),
                    pl.BlockSpec((1, tn), lambda i, j: (0, nd + j))],
        out_specs=pl.BlockSpec((ROW_TILE, tn), lambda i, j: (i, j)),
        compiler_params=_params("arbitrary", "arbitrary"),
        name="merge",
    )(*o_parts, *y_parts, w_attn_o, w_lru_o, proj, proj, b_gate, b_gate)


def _out_proj_kernel(*refs, x_tiles):
    a_ref, w_ref, *x_refs, o_ref = refs
    x = _select_part(pl.program_id(0), x_refs, x_tiles)
    o_ref[...] = x + jnp.dot(a_ref[...], w_ref[...].astype(BF16), preferred_element_type=F32)


def _out_proj(a, w, x_parts, *, tn=512):
    m, k = a.shape
    n = w.shape[1]
    return pl.pallas_call(
        functools.partial(_out_proj_kernel, x_tiles=_part_tiles(x_parts, ROW_TILE)),
        out_shape=jax.ShapeDtypeStruct((m, n), F32),
        grid=(m // ROW_TILE, n // tn),
        in_specs=[pl.BlockSpec((ROW_TILE, k), lambda i, j: (i, 0)),
                  pl.BlockSpec((k, tn), lambda i, j: (0, j))]
                 + _stacked_row_specs(x_parts, ROW_TILE, tn, lambda i, j: j),
        out_specs=pl.BlockSpec((ROW_TILE, tn), lambda i, j: (i, j)),
        compiler_params=_params("arbitrary", "arbitrary"),
        name="out_proj",
    )(a, w, *x_parts)


def _swiglu_down_kernel(a0_ref, a1_ref, w_ref, x_ref, *o_refs, o_tiles):
    i, k = pl.program_id(0), pl.program_id(2)
    w = w_ref[...].astype(BF16)

    def emit(a_ref, first):
        part = jnp.dot(a_ref[...], w, preferred_element_type=F32)
        lo = 0
        for o_ref, n in zip(o_refs, o_tiles):
            @pl.when((i >= lo) & (i < lo + n))
            def _():
                if first:
                    o_ref[...] = x_ref[...] + part
                else:
                    o_ref[...] += part
            lo += n

    @pl.when(k == 0)
    def _():
        emit(a0_ref, True)

    @pl.when(k == 1)
    def _():
        emit(a1_ref, False)


def _swiglu_down(a, w, x, out_rows, *, tn=256):
    m, k = a.shape
    n = w.shape[1]
    tk = k // 2
    nj = n // tn
    o_tiles = tuple(r // ROW_TILE for r in out_rows)
    once = dict(pipeline_mode=pl.Buffered(1))

    def out_spec(lo, cnt):
        def index(i, j, s):
            mine = (i >= lo) & (i < lo + cnt)
            return (jnp.clip(i - lo, 0, cnt - 1), jnp.where(mine, j, jnp.where(i < lo, 0, nj - 1)))
        return pl.BlockSpec((ROW_TILE, tn), index)

    out_specs, lo = [], 0
    for cnt in o_tiles:
        out_specs.append(out_spec(lo, cnt))
        lo += cnt
    return pl.pallas_call(
        functools.partial(_swiglu_down_kernel, o_tiles=o_tiles),
        out_shape=tuple(jax.ShapeDtypeStruct((r, n), F32) for r in out_rows),
        grid=(m // ROW_TILE, nj, 2),
        in_specs=[pl.BlockSpec((ROW_TILE, tk), lambda i, j, s: (i, 0), **once),
                  pl.BlockSpec((ROW_TILE, tk), lambda i, j, s: (i, 1), **once),
                  pl.BlockSpec((tk, tn), lambda i, j, s: (s, j)),
                  pl.BlockSpec((ROW_TILE, tn), lambda i, j, s: (i, j))],
        out_specs=tuple(out_specs),
        compiler_params=_params("arbitrary", "arbitrary", "arbitrary"),
        name="swiglu_down",
    )(a, a, w, x)


def _swiglu_up_kernel(x_ref, wg_ref, wu_ref, o_ref):
    x = x_ref[...]
    g = jnp.dot(x, wg_ref[...].astype(BF16), preferred_element_type=F32)
    u = jnp.dot(x, wu_ref[...].astype(BF16), preferred_element_type=F32)
    o_ref[...] = (jax.nn.silu(g) * u).astype(o_ref.dtype)


def _swiglu_up(xn, wg, wu, *, tn=256):
    m, d = xn.shape
    f = wg.shape[1]
    return pl.pallas_call(
        _swiglu_up_kernel,
        out_shape=jax.ShapeDtypeStruct((m, f), BF16),
        grid=(m // ROW_TILE, f // tn),
        in_specs=[pl.BlockSpec((ROW_TILE, d), lambda i, j: (i, 0)),
                  pl.BlockSpec((d, tn), lambda i, j: (0, j)),
                  pl.BlockSpec((d, tn), lambda i, j: (0, j))],
        out_specs=pl.BlockSpec((ROW_TILE, tn), lambda i, j: (i, j)),
        compiler_params=_params("parallel", "arbitrary"),
        name="swiglu_up",
    )(xn, wg, wu)


def kernel(x_prompt, x_sample, cache_k, cache_v, state_h, state_conv, page_table, norm_mix, w_in,
           b_gate, q_norm, k_norm, lambda_q1, lambda_k1, lambda_q2, lambda_k2, rel_bias,
           attn_subln, w_attn_o, conv_w, conv_b, w_r, b_r, w_i, b_i, lru_lambda, w_lru_o, w_out,
           norm_ffn, w_ffn_gate, w_ffn_up, w_ffn_down):
    bp, tp, d = x_prompt.shape
    bs, ts, _ = x_sample.shape
    depth = w_in.shape[0]
    assert depth == 1
    l = 0
    mp, ms = bp * tp, bs * ts
    qk_w = N_HEADS * HEAD_W
    lru_w = conv_b.shape[1]
    n_pages, page = page_table.shape[1], cache_k.shape[2]
    past = n_pages * page
    col_v, col_xr, col_xg, col_gate = 2 * qk_w, 3 * qk_w, 3 * qk_w + lru_w, 3 * qk_w + 2 * lru_w

    x_parts = (x_prompt.reshape(mp, d), x_sample.reshape(ms, d))

    xn = _rmsnorm(x_parts, norm_mix[l])
    qk_gain = jnp.concatenate([jnp.tile(q_norm[l], 2 * N_HEADS), jnp.tile(k_norm[l], 2 * N_HEADS)])
    proj = _in_proj(xn, w_in[l], qk_gain.reshape(1, 2 * qk_w))

    pattern_pages, page_pattern = _page_patterns(past, page, n_pages, ts)
    bias_tiles, bias_far, bias_pages, bias_new = _bias_tables(rel_bias, past, page, ts, pattern_pages)
    lams = [v[l].reshape(1, D_HEAD) for v in (lambda_q1, lambda_k1, lambda_q2, lambda_k2)]
    subln = attn_subln[l]
    o_prompt = _prompt_attention(proj, bp, tp, bias_tiles, bias_far,
                                 subln.reshape(N_HEADS, 1, HEAD_W), lams)
    as_head_rows = lambda a: a.reshape(ms * N_HEADS, HEAD_W)
    o_sample = _sample_attention(as_head_rows(proj[mp:, :qk_w]), as_head_rows(proj[mp:, qk_w:2 * qk_w]),
                                 as_head_rows(proj[mp:, col_v:col_v + qk_w]), bs,
                                 cache_k, cache_v, l, page_table, jnp.asarray(page_pattern),
                                 bias_pages, bias_new, jnp.tile(subln, (ts, 1)), lams).reshape(ms, qk_w)

    lru_weights = (conv_w[l], conv_b[l].reshape(1, lru_w), w_r[l], b_r[l].reshape(1, lru_w),
                   w_i[l], b_i[l].reshape(1, lru_w), lru_lambda[l].reshape(1, lru_w))
    y_prompt, h_prompt = _lru_prompt(proj, col_xr, col_xg, bp, tp, lru_weights)
    prev_rows = jnp.pad(state_conv[l], ((0, 0), (ts - (CONV_WIDTH - 1), 0), (0, 0))).reshape(ms, lru_w)
    h0_rows = jnp.repeat(state_h[l], ts, axis=0)
    y_sample, h_rows = _lru_sample(proj, mp, col_xr, col_xg, ms, ts, prev_rows, h0_rows, lru_weights)

    merged = _merge((o_prompt, o_sample), (y_prompt, y_sample), w_attn_o[l], w_lru_o[l],
                    proj, col_gate, b_gate[l].reshape(1, 2 * d))
    x1 = _out_proj(merged, w_out[l], x_parts)
    xn2 = _rmsnorm((x1,), norm_ffn[l])
    hid = _swiglu_up(xn2, w_ffn_gate[l], w_ffn_up[l])
    out_prompt, out_sample = _swiglu_down(hid, w_ffn_down[l], x1, (mp, ms))

    k_all = proj[:, qk_w:2 * qk_w]
    v_all = proj[:, col_v:col_v + qk_w]
    xr_all = proj[:, col_xr:col_xr + lru_w]
    tail = CONV_WIDTH - 1
    return (out_prompt.reshape(bp, tp, d), out_sample.reshape(bs, ts, d),
            k_all[:mp].reshape(1, bp, tp, N_HEADS, HEAD_W), v_all[:mp].reshape(1, bp, tp, N_HEADS, HEAD_W),
            h_prompt[None].astype(state_h.dtype),
            xr_all[:mp].reshape(bp, tp, lru_w)[None, :, tp - tail:, :].astype(state_conv.dtype),
            k_all[mp:].reshape(1, bs, ts, N_HEADS, HEAD_W), v_all[mp:].reshape(1, bs, ts, N_HEADS, HEAD_W),
            h_rows.reshape(bs, ts, lru_w)[None, :, ts - 1, :].astype(state_h.dtype),
            xr_all[mp:].reshape(bs, ts, lru_w)[None, :, ts - tail:, :].astype(state_conv.dtype))
```
